```python
import jax, jax.numpy as jnp
from jax import lax
import numpy as np

D_MODEL = 2048
BATCH = 4
SEQ = 8192
DEPTH = 4

D_FF = 4096
GM_WIDTH = 1024
GM_GROUPS = 4
CHUNK = 128
CONV_WIDTH = 1024
CONV_K = 31
N_HEADS = 16
Q_RANK = 512
KV_RANK = 512
NOPE_DIM = 128
ROPE_DIM = 64
V_DIM = 128
ROPE_THETA = 10000.0
Q_BLOCK = 128
MLA_WIDTH = N_HEADS * V_DIM
N_BRANCH = 3
MIX_WIDTH = GM_WIDTH + CONV_WIDTH + MLA_WIDTH
IN_SPLITS = (GM_WIDTH, GM_WIDTH, CONV_WIDTH, CONV_WIDTH, Q_RANK, KV_RANK, ROPE_DIM, N_BRANCH * D_MODEL)
IN_OFFSETS = tuple(int(o) for o in np.cumsum(IN_SPLITS)[:-1])
N_IN = int(sum(IN_SPLITS))
EPS = 1e-6

kernel_name = "hybrid_gmlp_conv_mla_macaron"


def rms_norm(x, g):
    xf = x.astype(jnp.float32)
    y = xf * lax.rsqrt(jnp.mean(xf * xf, axis=-1, keepdims=True) + EPS)
    return (y * g.astype(jnp.float32)).astype(x.dtype)


def layer_norm(x, g, b):
    xf = x.astype(jnp.float32)
    mu = jnp.mean(xf, axis=-1, keepdims=True)
    xc = xf - mu
    y = xc * lax.rsqrt(jnp.mean(xc * xc, axis=-1, keepdims=True) + EPS)
    return (y * g.astype(jnp.float32) + b.astype(jnp.float32)).astype(x.dtype)


def swiglu_ffn(h, w_in, w_out):
    gate, up = jnp.split(h @ w_in, 2, axis=-1)
    return (jax.nn.silu(gate) * up) @ w_out


def apply_rope(x, cos, sin):
    xf = x.astype(jnp.float32)
    x1, x2 = jnp.split(xf, 2, axis=-1)
    return jnp.concatenate([x1 * cos - x2 * sin, x2 * cos + x1 * sin], axis=-1).astype(x.dtype)


def chunked_spatial_gating(u, v, ln_g, ln_b, w_s, b_s):
    v = layer_norm(v, ln_g, ln_b)
    bsz, s_len, _ = v.shape
    n_chunks = s_len // CHUNK
    vg = v.reshape(bsz, n_chunks, CHUNK, GM_GROUPS, GM_WIDTH // GM_GROUPS)
    causal = jnp.tril(jnp.ones((CHUNK, CHUNK), dtype=bool))
    w = jnp.where(causal[None], w_s, jnp.zeros_like(w_s)).astype(vg.dtype)
    s = jnp.einsum('gts,bcsgd->bctgd', w, vg) + b_s.T[None, None, :, :, None].astype(vg.dtype)
    return u * s.reshape(bsz, s_len, GM_WIDTH)


def conformer_conv(val, gate, conv_w, conv_b, ln_g, ln_b):
    a = val * jax.nn.sigmoid(gate)
    y = lax.conv_general_dilated(
        a, conv_w[:, None, :].astype(a.dtype), window_strides=(1,),
        padding=[(CONV_K - 1, 0)], dimension_numbers=('NWC', 'WIO', 'NWC'),
        feature_group_count=CONV_WIDTH) + conv_b.astype(a.dtype)
    return jax.nn.silu(layer_norm(y, ln_g, ln_b))


def latent_attention(c_q, c_kv, k_rope, cos, sin, q_norm, w_uq, kv_norm, w_ukv):
    bsz, s_len, _ = c_q.shape
    q = (rms_norm(c_q, q_norm) @ w_uq).reshape(bsz, s_len, N_HEADS, NOPE_DIM + ROPE_DIM)
    q_nope = q[..., :NOPE_DIM]
    q_rot = apply_rope(q[..., NOPE_DIM:], cos[:, :, None, :], sin[:, :, None, :])
    k_rot = apply_rope(k_rope, cos, sin)
    kv = (rms_norm(c_kv, kv_norm) @ w_ukv).reshape(bsz, s_len, N_HEADS, NOPE_DIM + V_DIM)
    k_nope = kv[..., :NOPE_DIM]
    v = kv[..., NOPE_DIM:]
    scale = (NOPE_DIM + ROPE_DIM) ** -0.5
    n_blocks = s_len // Q_BLOCK
    qn_b = q_nope.reshape(bsz, n_blocks, Q_BLOCK, N_HEADS, NOPE_DIM).transpose(1, 0, 2, 3, 4)
    qr_b = q_rot.reshape(bsz, n_blocks, Q_BLOCK, N_HEADS, ROPE_DIM).transpose(1, 0, 2, 3, 4)
    key_pos = jnp.arange(s_len)

    def one_block(args):
        qn, qr, start = args
        s = (jnp.einsum('bqhd,bkhd->bhqk', qn, k_nope)
             + jnp.einsum('bqhr,bkr->bhqk', qr, k_rot)).astype(jnp.float32) * scale
        q_pos = start + jnp.arange(Q_BLOCK)
        s = jnp.where(key_pos[None, :] <= q_pos[:, None], s, -jnp.inf)
        p = jax.nn.softmax(s, axis=-1).astype(v.dtype)
        return jnp.einsum('bhqk,bkhd->bqhd', p, v)

    o = lax.map(one_block, (qn_b, qr_b, jnp.arange(n_blocks) * Q_BLOCK))
    return o.transpose(1, 0, 2, 3, 4).reshape(bsz, s_len, MLA_WIDTH)


def hybrid_mixer(h, cos, sin, w_in, gm_ln_g, gm_ln_b, gm_w_s, gm_b_s,
                 conv_w, conv_b, conv_ln_g, conv_ln_b,
                 q_norm, w_uq, kv_norm, w_ukv, w_branch, w_out):
    a_u, a_v, b_val, b_gate, c_q, c_kv, k_rope, gate_logits = jnp.split(h @ w_in, IN_OFFSETS, axis=-1)
    o_a = chunked_spatial_gating(jax.nn.gelu(a_u), jax.nn.gelu(a_v), gm_ln_g, gm_ln_b, gm_w_s, gm_b_s)
    o_b = conformer_conv(b_val, b_gate, conv_w, conv_b, conv_ln_g, conv_ln_b)
    o_c = latent_attention(c_q, c_kv, k_rope, cos, sin, q_norm, w_uq, kv_norm, w_ukv)
    y_a = o_a @ w_branch[:GM_WIDTH]
    y_b = o_b @ w_branch[GM_WIDTH:GM_WIDTH + CONV_WIDTH]
    y_c = o_c @ w_branch[GM_WIDTH + CONV_WIDTH:]
    g = jax.nn.sigmoid(gate_logits.reshape(h.shape[0], h.shape[1], N_BRANCH, D_MODEL))
    merged = g[:, :, 0] * y_a + g[:, :, 1] * y_b + g[:, :, 2] * y_c
    return merged @ w_out


def setup_inputs(seed: int = 0) -> dict:
    key = jax.random.key(seed)
    ks = iter(jax.random.split(key, 32))
    L = DEPTH

    def nrm(shape, scale):
        return jax.random.normal(next(ks), shape, jnp.float32) * scale

    def gain(shape):
        return 1.0 + 0.05 * jax.random.normal(next(ks), shape, jnp.float32)

    x = jax.random.normal(next(ks), (BATCH, SEQ, D_MODEL), jnp.float32)
    offsets = jax.random.randint(next(ks), (BATCH, 1), 0, 4096, dtype=jnp.int32)
    positions = (jnp.arange(SEQ, dtype=jnp.int32)[None, :] + offsets).astype(jnp.int32)
    return {
        "x": x,
        "positions": positions,
        "ffn1_norm_pre": gain((L, D_MODEL)),
        "ffn1_norm_post": gain((L, D_MODEL)),
        "ffn1_w_in": nrm((L, D_MODEL, 2 * D_FF), D_MODEL ** -0.5),
        "ffn1_w_out": nrm((L, D_FF, D_MODEL), D_FF ** -0.5),
        "mix_norm_pre": gain((L, D_MODEL)),
        "mix_norm_post": gain((L, D_MODEL)),
        "mix_w_in": nrm((L, D_MODEL, N_IN), D_MODEL ** -0.5),
        "gm_ln_g": gain((L, GM_WIDTH)),
        "gm_ln_b": nrm((L, GM_WIDTH), 0.02),
        "gm_w_s": nrm((L, GM_GROUPS, CHUNK, CHUNK), CHUNK ** -0.5),
        "gm_b_s": gain((L, GM_GROUPS, CHUNK)),
        "conv_w": nrm((L, CONV_K, CONV_WIDTH), CONV_K ** -0.5),
        "conv_b": nrm((L, CONV_WIDTH), 0.02),
        "conv_ln_g": gain((L, CONV_WIDTH)),
        "conv_ln_b": nrm((L, CONV_WIDTH), 0.02),
        "mla_q_norm": gain((L, Q_RANK)),
        "mla_w_uq": nrm((L, Q_RANK, N_HEADS * (NOPE_DIM + ROPE_DIM)), Q_RANK ** -0.5),
        "mla_kv_norm": gain((L, KV_RANK)),
        "mla_w_ukv": nrm((L, KV_RANK, N_HEADS * (NOPE_DIM + V_DIM)), KV_RANK ** -0.5),
        "mix_w_branch": nrm((L, MIX_WIDTH, D_MODEL), (MIX_WIDTH // N_BRANCH) ** -0.5),
        "mix_w_out": nrm((L, D_MODEL, D_MODEL), D_MODEL ** -0.5),
        "ffn2_norm_pre": gain((L, D_MODEL)),
        "ffn2_norm_post": gain((L, D_MODEL)),
        "ffn2_w_in": nrm((L, D_MODEL, 2 * D_FF), D_MODEL ** -0.5),
        "ffn2_w_out": nrm((L, D_FF, D_MODEL), D_FF ** -0.5),
    }


def reference(x, positions, ffn1_norm_pre, ffn1_norm_post, ffn1_w_in, ffn1_w_out,
              mix_norm_pre, mix_norm_post, mix_w_in, gm_ln_g, gm_ln_b, gm_w_s, gm_b_s,
              conv_w, conv_b, conv_ln_g, conv_ln_b,
              mla_q_norm, mla_w_uq, mla_kv_norm, mla_w_ukv, mix_w_branch, mix_w_out,
              ffn2_norm_pre, ffn2_norm_post, ffn2_w_in, ffn2_w_out):
    inv_freq = ROPE_THETA ** (-jnp.arange(0, ROPE_DIM, 2, dtype=jnp.float32) / ROPE_DIM)
    ang = positions.astype(jnp.float32)[..., None] * inv_freq
    cos, sin = jnp.cos(ang), jnp.sin(ang)
    for l in range(DEPTH):
        h = rms_norm(x, ffn1_norm_pre[l])
        x = x + 0.5 * rms_norm(swiglu_ffn(h, ffn1_w_in[l], ffn1_w_out[l]), ffn1_norm_post[l])
        h = rms_norm(x, mix_norm_pre[l])
        m = hybrid_mixer(h, cos, sin, mix_w_in[l], gm_ln_g[l], gm_ln_b[l], gm_w_s[l], gm_b_s[l],
                         conv_w[l], conv_b[l], conv_ln_g[l], conv_ln_b[l],
                         mla_q_norm[l], mla_w_uq[l], mla_kv_norm[l], mla_w_ukv[l],
                         mix_w_branch[l], mix_w_out[l])
        x = x + rms_norm(m, mix_norm_post[l])
        h = rms_norm(x, ffn2_norm_pre[l])
        x = x + 0.5 * rms_norm(swiglu_ffn(h, ffn2_w_in[l], ffn2_w_out[l]), ffn2_norm_post[l])
    return x
```

```python
import functools
import math

import jax
import jax.numpy as jnp
import numpy as np
from jax import lax
from jax.experimental import pallas as pl
from jax.experimental.pallas import tpu as pltpu

D_MODEL = 2048
D_FF = 4096
GM_WIDTH = 1024
GM_GROUPS = 4
CHUNK = 128
CONV_WIDTH = 1024
CONV_K = 31
N_HEADS = 16
Q_RANK = 512
KV_RANK = 512
NOPE_DIM = 128
ROPE_DIM = 64
V_DIM = 128
ROPE_THETA = 10000.0
EPS = 1e-6

LANES = 128
HEAD_QK = NOPE_DIM + 2 * ROPE_DIM
CONV_HALO = 32

FFN_TM = 512
FFN_TF = 512
PROJ_TM = 1024
PROJ_TN = 1280
GM_TM = 256
CONV_TM = 256
QKV_TM = 256
ATT_TQ = 512
ATT_TK = 512
MERGE_TM = 256
ROPE_TM = 1024

VMEM_LIMIT = 56 * 1024 * 1024

F32 = jnp.float32
BF16 = jnp.bfloat16
NEG_BIG = -1e30


def _params(*sem):
    return pltpu.CompilerParams(dimension_semantics=sem, vmem_limit_bytes=VMEM_LIMIT)


def _rms(x, g):
    return x * lax.rsqrt(jnp.mean(x * x, axis=-1, keepdims=True) + EPS) * g


def _layer_norm(x, g, b):
    mu = jnp.mean(x, axis=-1, keepdims=True)
    xc = x - mu
    return xc * lax.rsqrt(jnp.mean(xc * xc, axis=-1, keepdims=True) + EPS) * g + b


def _sigmoid(x):
    return 1.0 / (1.0 + jnp.exp(-x))


def _gelu_tanh(x):
    c = math.sqrt(2.0 / math.pi)
    return x * (0.5 * (1.0 + jnp.tanh(c * (x + 0.044715 * (x * x * x)))))


def _dot(a, b):
    return jnp.dot(a, b, preferred_element_type=F32)


def _rope_table_kernel(pos_ref, freq_ref, o_ref):
    ang = pos_ref[...] * freq_ref[...]
    lane = lax.broadcasted_iota(jnp.int32, ang.shape, 1)
    o_ref[...] = jnp.where(lane < ROPE_DIM, jnp.cos(ang), jnp.sin(ang))


def _rope_table(pos_f, freq4):
    t = pos_f.shape[0]
    tm = min(ROPE_TM, t)
    return pl.pallas_call(
        _rope_table_kernel,
        grid=(t // tm,),
        in_specs=[pl.BlockSpec((tm, 1), lambda i: (i, 0)),
                  pl.BlockSpec((1, 2 * ROPE_DIM), lambda i: (0, 0))],
        out_specs=pl.BlockSpec((tm, 2 * ROPE_DIM), lambda i: (i, 0)),
        out_shape=jax.ShapeDtypeStruct((t, 2 * ROPE_DIM), F32),
        compiler_params=_params("parallel"),
        name="rope_table",
    )(pos_f, freq4)


def _ffn_kernel(x_ref, gpre_ref, wg_ref, wu_ref, wo_ref, gpost_ref, o_ref, h_ref, acc_ref):
    f = pl.program_id(1)

    @pl.when(f == 0)
    def _():
        h_ref[...] = _rms(x_ref[...], gpre_ref[...]).astype(BF16)
        acc_ref[...] = jnp.zeros_like(acc_ref)

    h = h_ref[...]
    gate = _dot(h, wg_ref[...])
    up = _dot(h, wu_ref[...])
    act = (gate * _sigmoid(gate) * up).astype(BF16)
    acc_ref[...] += _dot(act, wo_ref[...])

    @pl.when(f == pl.num_programs(1) - 1)
    def _():
        o_ref[...] = x_ref[...] + 0.5 * _rms(acc_ref[...], gpost_ref[...])


def _ffn(x, gpre, w_in, w_out, gpost, layer):
    t, d = x.shape
    d_ff = w_out.shape[1]
    tm, tf = min(FFN_TM, t), min(FFN_TF, d_ff)
    nf = d_ff // tf
    return pl.pallas_call(
        _ffn_kernel,
        grid=(t // tm, nf),
        in_specs=[
            pl.BlockSpec((tm, d), lambda i, f: (i, 0)),
            pl.BlockSpec((None, 1, d), lambda i, f: (layer, 0, 0)),
            pl.BlockSpec((None, d, tf), lambda i, f: (layer, 0, f)),
            pl.BlockSpec((None, d, tf), lambda i, f: (layer, 0, f + nf)),
            pl.BlockSpec((None, tf, d), lambda i, f: (layer, f, 0)),
            pl.BlockSpec((None, 1, d), lambda i, f: (layer, 0, 0)),
        ],
        out_specs=pl.BlockSpec((tm, d), lambda i, f: (i, 0)),
        out_shape=jax.ShapeDtypeStruct((t, d), F32),
        scratch_shapes=[pltpu.VMEM((tm, d), BF16), pltpu.VMEM((tm, d), F32)],
        compiler_params=_params("parallel", "arbitrary"),
        name="ffn",
    )(x, gpre, w_in, w_in, w_out, gpost)


def _proj_kernel(x_ref, g_ref, w_ref, o_ref, h_ref):
    @pl.when(pl.program_id(1) == 0)
    def _():
        h_ref[...] = _rms(x_ref[...], g_ref[...]).astype(BF16)

    o_ref[...] = _dot(h_ref[...], w_ref[...]).astype(BF16)


def _proj(x, g, w_ext, layer):
    t, d = x.shape
    n = w_ext.shape[2]
    tm, tn = min(PROJ_TM, t), min(PROJ_TN, n)
    return pl.pallas_call(
        _proj_kernel,
        grid=(t // tm, n // tn),
        in_specs=[
            pl.BlockSpec((tm, d), lambda i, j: (i, 0)),
            pl.BlockSpec((None, 1, d), lambda i, j: (layer, 0, 0)),
            pl.BlockSpec((None, d, tn), lambda i, j: (layer, 0, j)),
        ],
        out_specs=pl.BlockSpec((tm, tn), lambda i, j: (i, j)),
        out_shape=jax.ShapeDtypeStruct((t, n), BF16),
        scratch_shapes=[pltpu.VMEM((tm, d), BF16)],
        compiler_params=_params("parallel", "arbitrary"),
        name="mix_proj",
    )(x, g, w_ext)


def _gmlp_kernel(u_ref, v_ref, lng_ref, lnb_ref, ws_ref, bias_ref, o_ref):
    tm, width = u_ref.shape
    gw = width // GM_GROUPS
    vn = _layer_norm(_gelu_tanh(v_ref[...].astype(F32)), lng_ref[...], lnb_ref[...]).astype(BF16)
    row = lax.broadcasted_iota(jnp.int32, (CHUNK, CHUNK), 0)
    col = lax.broadcasted_iota(jnp.int32, (CHUNK, CHUNK), 1)
    for g in range(GM_GROUPS):
        w = jnp.where(col <= row, ws_ref[g], 0.0).astype(BF16)
        cols = slice(g * gw, (g + 1) * gw)
        for c in range(tm // CHUNK):
            rows = slice(c * CHUNK, (c + 1) * CHUNK)
            s = _dot(w, vn[rows, cols]) + bias_ref[:, cols]
            o_ref[rows, cols] = (_gelu_tanh(u_ref[rows, cols].astype(F32)) * s).astype(BF16)


def _gmlp(p, lng, lnb, w_s, bias_full, layer):
    t = p.shape[0]
    width = lng.shape[2]
    tm = min(GM_TM, t)
    return pl.pallas_call(
        _gmlp_kernel,
        grid=(t // tm,),
        in_specs=[
            pl.BlockSpec((tm, width), lambda i: (i, 0)),
            pl.BlockSpec((tm, width), lambda i: (i, 1)),
            pl.BlockSpec((None, 1, width), lambda i: (layer, 0, 0)),
            pl.BlockSpec((None, 1, width), lambda i: (layer, 0, 0)),
            pl.BlockSpec((None, GM_GROUPS, CHUNK, CHUNK), lambda i: (layer, 0, 0, 0)),
            pl.BlockSpec((None, CHUNK, width), lambda i: (layer, 0, 0)),
        ],
        out_specs=pl.BlockSpec((tm, width), lambda i: (i, 0)),
        out_shape=jax.ShapeDtypeStruct((t, width), BF16),
        compiler_params=_params("parallel"),
        name="gmlp",
    )(p, p, lng, lnb, w_s, bias_full)


def _conv_kernel(val_ref, gate_ref, w_ref, b_ref, lng_ref, lnb_ref, o_ref, ext_ref):
    tm = val_ref.shape[0]
    i = pl.program_id(1)

    @pl.when(i == 0)
    def _():
        ext_ref[0:CONV_HALO, :] = jnp.zeros((CONV_HALO, ext_ref.shape[1]), F32)

    @pl.when(i > 0)
    def _():
        ext_ref[0:CONV_HALO, :] = ext_ref[tm:tm + CONV_HALO, :]

    ext_ref[CONV_HALO:CONV_HALO + tm, :] = val_ref[...].astype(F32) * _sigmoid(gate_ref[...].astype(F32))
    first = CONV_HALO - (CONV_K - 1)
    y = jnp.zeros(val_ref.shape, F32) + b_ref[...]
    for k in range(CONV_K):
        y = y + w_ref[k:k + 1, :] * ext_ref[first + k:first + k + tm, :]
    z = _layer_norm(y, lng_ref[...], lnb_ref[...])
    o_ref[...] = (z * _sigmoid(z)).astype(BF16)


def _conv(p, w, b, lng, lnb, layer, batch):
    t = p.shape[0]
    width = w.shape[2]
    seq = t // batch
    tm = min(CONV_TM, seq)
    ns = seq // tm
    vblk = 2 * GM_WIDTH // width
    return pl.pallas_call(
        _conv_kernel,
        grid=(batch, ns),
        in_specs=[
            pl.BlockSpec((tm, width), lambda bi, i: (bi * ns + i, vblk)),
            pl.BlockSpec((tm, width), lambda bi, i: (bi * ns + i, vblk + 1)),
            pl.BlockSpec((None, CONV_K, width), lambda bi, i: (layer, 0, 0)),
            pl.BlockSpec((None, 1, width), lambda bi, i: (layer, 0, 0)),
            pl.BlockSpec((None, 1, width), lambda bi, i: (layer, 0, 0)),
            pl.BlockSpec((None, 1, width), lambda bi, i: (layer, 0, 0)),
        ],
        out_specs=pl.BlockSpec((tm, width), lambda bi, i: (bi * ns + i, 0)),
        out_shape=jax.ShapeDtypeStruct((t, width), BF16),
        scratch_shapes=[pltpu.VMEM((tm + CONV_HALO, width), F32)],
        compiler_params=_params("arbitrary", "arbitrary"),
        name="conv",
    )(p, p, w, b, lng, lnb)


def _rope_pair(pair, cs):
    z = pair * cs
    return z + pltpu.roll(z, ROPE_DIM, axis=1)


def _q_kernel(c_ref, cs_ref, g_ref, w_ref, o_ref, *, q_scale):
    cn = _rms(c_ref[...].astype(F32), g_ref[...]).astype(BF16)
    q = _dot(cn, w_ref[...]) * q_scale
    cs = cs_ref[...]
    for h in range(o_ref.shape[1]):
        base = h * HEAD_QK
        o_ref[0, h, :, 0:NOPE_DIM] = q[:, base:base + NOPE_DIM].astype(BF16)
        rot = _rope_pair(q[:, base + NOPE_DIM:base + HEAD_QK], cs)
        o_ref[0, h, :, NOPE_DIM:HEAD_QK] = rot.astype(BF16)


def _kv_kernel(c_ref, kr_ref, cs_ref, g_ref, w_ref, k_ref, v_ref):
    cn = _rms(c_ref[...].astype(F32), g_ref[...]).astype(BF16)
    kv = _dot(cn, w_ref[...])
    rot = _rope_pair(kr_ref[...].astype(F32), cs_ref[...])
    lane = lax.broadcasted_iota(jnp.int32, rot.shape, 1)
    rot = jnp.where(lane < ROPE_DIM, rot, 0.0).astype(BF16)
    hw = NOPE_DIM + V_DIM
    for h in range(k_ref.shape[1]):
        k_ref[0, h, :, 0:NOPE_DIM] = kv[:, h * hw:h * hw + NOPE_DIM].astype(BF16)
        k_ref[0, h, :, NOPE_DIM:HEAD_QK] = rot
        v_ref[0, h, :, :] = kv[:, h * hw + NOPE_DIM:(h + 1) * hw].astype(BF16)


def _q_proj(p, cs, g, w_q, layer, batch, col_blk):
    t = p.shape[0]
    rank = w_q.shape[1]
    heads = w_q.shape[2] // HEAD_QK
    seq = t // batch
    tm = min(QKV_TM, seq)
    ns = seq // tm
    q_scale = float((NOPE_DIM + ROPE_DIM) ** -0.5 * math.log2(math.e))
    return pl.pallas_call(
        functools.partial(_q_kernel, q_scale=q_scale),
        grid=(batch, ns),
        in_specs=[
            pl.BlockSpec((tm, rank), lambda bi, i: (bi * ns + i, col_blk)),
            pl.BlockSpec((tm, 2 * ROPE_DIM), lambda bi, i: (bi * ns + i, 0)),
            pl.BlockSpec((None, 1, rank), lambda bi, i: (layer, 0, 0)),
            pl.BlockSpec((None, rank, heads * HEAD_QK), lambda bi, i: (layer, 0, 0)),
        ],
        out_specs=pl.BlockSpec((1, heads, tm, HEAD_QK), lambda bi, i: (bi, 0, i, 0)),
        out_shape=jax.ShapeDtypeStruct((batch, heads, seq, HEAD_QK), BF16),
        compiler_params=_params("parallel", "parallel"),
        name="mla_q",
    )(p, cs, g, w_q)


def _kv_proj(p, cs, g, w_kv, layer, batch, col_blk, kr_blk):
    t = p.shape[0]
    rank = w_kv.shape[1]
    heads = w_kv.shape[2] // (NOPE_DIM + V_DIM)
    seq = t // batch
    tm = min(QKV_TM, seq)
    ns = seq // tm
    return pl.pallas_call(
        _kv_kernel,
        grid=(batch, ns),
        in_specs=[
            pl.BlockSpec((tm, rank), lambda bi, i: (bi * ns + i, col_blk)),
            pl.BlockSpec((tm, 2 * ROPE_DIM), lambda bi, i: (bi * ns + i, kr_blk)),
            pl.BlockSpec((tm, 2 * ROPE_DIM), lambda bi, i: (bi * ns + i, 0)),
            pl.BlockSpec((None, 1, rank), lambda bi, i: (layer, 0, 0)),
            pl.BlockSpec((None, rank, heads * (NOPE_DIM + V_DIM)), lambda bi, i: (layer, 0, 0)),
        ],
        out_specs=[
            pl.BlockSpec((1, heads, tm, HEAD_QK), lambda bi, i: (bi, 0, i, 0)),
            pl.BlockSpec((1, heads, tm, V_DIM), lambda bi, i: (bi, 0, i, 0)),
        ],
        out_shape=[
            jax.ShapeDtypeStruct((batch, heads, seq, HEAD_QK), BF16),
            jax.ShapeDtypeStruct((batch, heads, seq, V_DIM), BF16),
        ],
        compiler_params=_params("parallel", "parallel"),
        name="mla_kv",
    )(p, p, cs, g, w_kv)


def _flash_kernel(q_ref, k_ref, v_ref, o_ref, *, tq, tk):
    seq = q_ref.shape[2]
    row = lax.broadcasted_iota(jnp.int32, (tq, tk), 0)
    col = lax.broadcasted_iota(jnp.int32, (tq, tk), 1)

    def q_body(qi, carry):
        q0 = pl.multiple_of(qi * tq, tq)
        q = q_ref[0, 0, pl.ds(q0, tq), :]

        def kv_step(k0, state, diagonal):
            m, l, acc = state
            k = k_ref[0, 0, pl.ds(k0, tk), :]
            v = v_ref[0, 0, pl.ds(k0, tk), :]
            s = lax.dot_general(q, k, (((1,), (1,)), ((), ())), preferred_element_type=F32)
            if diagonal:
                s = jnp.where(col <= row, s, NEG_BIG)
            m_new = jnp.maximum(m, jnp.max(s, axis=-1, keepdims=True))
            alpha = jnp.exp2(m - m_new)
            pr = jnp.exp2(s - m_new)
            l_new = alpha * l + jnp.sum(pr, axis=-1, keepdims=True)
            acc_new = alpha * acc + _dot(pr.astype(BF16), v)
            return m_new, l_new, acc_new

        init = (jnp.full((tq, 1), NEG_BIG, F32), jnp.zeros((tq, 1), F32), jnp.zeros((tq, V_DIM), F32))
        state = lax.fori_loop(
            0, qi, lambda ki, st: kv_step(pl.multiple_of(ki * tk, tk), st, False), init)
        _, l, acc = kv_step(q0, state, True)
        o_ref[pl.ds(q0, tq), :] = (acc / l).astype(BF16)
        return carry

    lax.fori_loop(0, seq // tq, q_body, 0)


def _flash(q, k, v):
    batch, heads, seq, _ = q.shape
    tq = min(ATT_TQ, seq)
    assert ATT_TQ == ATT_TK
    return pl.pallas_call(
        functools.partial(_flash_kernel, tq=tq, tk=tq),
        grid=(batch, heads),
        in_specs=[
            pl.BlockSpec((1, 1, seq, HEAD_QK), lambda b, h: (b, h, 0, 0)),
            pl.BlockSpec((1, 1, seq, HEAD_QK), lambda b, h: (b, h, 0, 0)),
            pl.BlockSpec((1, 1, seq, V_DIM), lambda b, h: (b, h, 0, 0)),
        ],
        out_specs=pl.BlockSpec((seq, V_DIM), lambda b, h: (b, h)),
        out_shape=jax.ShapeDtypeStruct((batch * seq, heads * V_DIM), BF16),
        compiler_params=_params("parallel", "parallel"),
        name="mla_attn",
    )(q, k, v)


def _merge_kernel(oa_ref, ob_ref, oc_ref, g0_ref, g1_ref, g2_ref, x_ref,
                  wa_ref, wb_ref, wc_ref, wo_ref, gpost_ref, o_ref):
    merged = _sigmoid(g0_ref[...].astype(F32)) * _dot(oa_ref[...], wa_ref[...])
    merged += _sigmoid(g1_ref[...].astype(F32)) * _dot(ob_ref[...], wb_ref[...])
    merged += _sigmoid(g2_ref[...].astype(F32)) * _dot(oc_ref[...], wc_ref[...])
    m = _dot(merged.astype(BF16), wo_ref[...])
    o_ref[...] = x_ref[...] + _rms(m, gpost_ref[...])


def _merge(o_a, o_b, o_c, p, x, w_branch, w_out, gpost, layer):
    t, d = x.shape
    wa, wb, wc = o_a.shape[1], o_b.shape[1], o_c.shape[1]
    tm = min(MERGE_TM, t)
    gblk = (2 * wa + 2 * wb) // d
    once = pl.Buffered(1)
    return pl.pallas_call(
        _merge_kernel,
        grid=(t // tm,),
        in_specs=[
            pl.BlockSpec((tm, wa), lambda i: (i, 0)),
            pl.BlockSpec((tm, wb), lambda i: (i, 0)),
            pl.BlockSpec((tm, wc), lambda i: (i, 0)),
            pl.BlockSpec((tm, d), lambda i: (i, gblk)),
            pl.BlockSpec((tm, d), lambda i: (i, gblk + 1)),
            pl.BlockSpec((tm, d), lambda i: (i, gblk + 2)),
            pl.BlockSpec((tm, d), lambda i: (i, 0)),
            pl.BlockSpec((None, wa, d), lambda i: (layer, 0, 0), pipeline_mode=once),
            pl.BlockSpec((None, wb, d), lambda i: (layer, wa // wb, 0), pipeline_mode=once),
            pl.BlockSpec((None, wc, d), lambda i: (layer, (wa + wb) // wc, 0), pipeline_mode=once),
            pl.BlockSpec((None, d, d), lambda i: (layer, 0, 0), pipeline_mode=once),
            pl.BlockSpec((None, 1, d), lambda i: (layer, 0, 0)),
        ],
        out_specs=pl.BlockSpec((tm, d), lambda i: (i, 0)),
        out_shape=jax.ShapeDtypeStruct((t, d), F32),
        compiler_params=_params("parallel"),
        name="merge",
    )(o_a, o_b, o_c, p, p, p, x, w_branch, w_branch, w_branch, w_out, gpost)


def _rotate_half_columns(w):
    half = w.shape[-1] // 2
    return jnp.concatenate([-w[..., half:], w[..., :half]], axis=-1)


def _mix_in_layout(w_in):
    o = np.cumsum([0, GM_WIDTH, GM_WIDTH, CONV_WIDTH, CONV_WIDTH, Q_RANK, KV_RANK, ROPE_DIM])
    k_rope = w_in[..., o[6]:o[7]]
    body = [w_in[..., :o[4]], w_in[..., o[7]:], w_in[..., o[4]:o[6]], k_rope, _rotate_half_columns(k_rope)]
    n = sum(a.shape[-1] for a in body)
    tn = min(PROJ_TN, n)
    pad = (-n) % tn
    body.append(jnp.zeros(w_in.shape[:-1] + (pad,), w_in.dtype))
    return jnp.concatenate(body, axis=-1).astype(BF16)


def _q_layout(w_uq):
    lead = w_uq.shape[:-1]
    w = w_uq.reshape(lead + (N_HEADS, NOPE_DIM + ROPE_DIM))
    rope = w[..., NOPE_DIM:]
    w = jnp.concatenate([w[..., :NOPE_DIM], rope, _rotate_half_columns(rope)], axis=-1)
    return w.reshape(lead + (N_HEADS * HEAD_QK,)).astype(BF16)


def kernel(x, positions, ffn1_norm_pre, ffn1_norm_post, ffn1_w_in, ffn1_w_out, mix_norm_pre, mix_norm_post, mix_w_in, gm_ln_g, gm_ln_b, gm_w_s, gm_b_s, conv_w, conv_b, conv_ln_g, conv_ln_b, mla_q_norm, mla_w_uq, mla_kv_norm, mla_w_ukv, mix_w_branch, mix_w_out, ffn2_norm_pre, ffn2_norm_post, ffn2_w_in, ffn2_w_out):
    batch, seq, d = x.shape
    depth = ffn1_w_in.shape[0]
    t = batch * seq
    row = lambda a: a[:, None, :]

    inv_freq = ROPE_THETA ** (-jnp.arange(0, ROPE_DIM, 2, dtype=F32) / ROPE_DIM)
    freq4 = jnp.tile(inv_freq, 4)[None, :]
    cs = _rope_table(positions.astype(F32).reshape(t, 1), freq4)

    w_mix = _mix_in_layout(mix_w_in)
    w_q = _q_layout(mla_w_uq)
    w_kv = mla_w_ukv.astype(BF16)
    w_branch = mix_w_branch.astype(BF16)
    w_out = mix_w_out.astype(BF16)
    f1_in, f1_out = ffn1_w_in.astype(BF16), ffn1_w_out.astype(BF16)
    f2_in, f2_out = ffn2_w_in.astype(BF16), ffn2_w_out.astype(BF16)
    gm_bias = jnp.repeat(jnp.swapaxes(gm_b_s, 1, 2), GM_WIDTH // GM_GROUPS, axis=2)

    gates_end = 2 * GM_WIDTH + 2 * CONV_WIDTH + 3 * d
    cq_blk = gates_end // Q_RANK
    ckv_blk = (gates_end + Q_RANK) // KV_RANK
    kr_blk = (gates_end + Q_RANK + KV_RANK) // (2 * ROPE_DIM)

    xf = x.reshape(t, d)
    for l in range(depth):
        xf = _ffn(xf, row(ffn1_norm_pre), f1_in, f1_out, row(ffn1_norm_post), l)
        p = _proj(xf, row(mix_norm_pre), w_mix, l)
        o_a = _gmlp(p, row(gm_ln_g), row(gm_ln_b), gm_w_s, gm_bias, l)
        o_b = _conv(p, conv_w, row(conv_b), row(conv_ln_g), row(conv_ln_b), l, batch)
        q = _q_proj(p, cs, row(mla_q_norm), w_q, l, batch, cq_blk)
        k, v = _kv_proj(p, cs, row(mla_kv_norm), w_kv, l, batch, ckv_blk, kr_blk)
        o_c = _flash(q, k, v)
        xf = _merge(o_a, o_b, o_c, p, xf, w_branch, w_out, row(mix_norm_post), l)
        xf = _ffn(xf, row(ffn2_norm_pre), f2_in, f2_out, row(ffn2_norm_post), l)
    return xf.reshape(batch, seq, d)
```

```python
import functools
import math

import jax
import jax.numpy as jnp
import numpy as np
from jax import lax
from jax.experimental import pallas as pl
from jax.experimental.pallas import tpu as pltpu

D_MODEL = 2048
D_FF = 4096
GM_WIDTH = 1024
GM_GROUPS = 4
CHUNK = 128
CONV_WIDTH = 1024
CONV_K = 31
N_HEADS = 16
Q_RANK = 512
KV_RANK = 512
NOPE_DIM = 128
ROPE_DIM = 64
V_DIM = 128
ROPE_THETA = 10000.0
EPS = 1e-6

LANES = 128
SUBLANES = 8
CONV_RB = 256
HEAD_QK = NOPE_DIM + 2 * ROPE_DIM
V_EXT = V_DIM + 16
CONV_HALO = 32

FFN_TM = 512
FFN_TF = 1024
PROJ_TM = 1024
PROJ_TN = 1280
GM_TM = 256
CONV_TM = 256
QKV_TM = 256
ATT_T = 512
MERGE_TM = 256
ROPE_TM = 1024

VMEM_LIMIT = 56 * 1024 * 1024

F32 = jnp.float32
BF16 = jnp.bfloat16
NEG_BIG = -1e30


def _params(*sem):
    return pltpu.CompilerParams(dimension_semantics=sem, vmem_limit_bytes=VMEM_LIMIT)


def _rms(x, g):
    return x * lax.rsqrt(jnp.mean(x * x, axis=-1, keepdims=True) + EPS) * g


def _layer_norm(x, g, b):
    mu = jnp.mean(x, axis=-1, keepdims=True)
    xc = x - mu
    return xc * lax.rsqrt(jnp.mean(xc * xc, axis=-1, keepdims=True) + EPS) * g + b


def _sigmoid(x):
    return 1.0 / (1.0 + jnp.exp(-x))


def _gelu_tanh(x):
    c = math.sqrt(2.0 / math.pi)
    return x * (0.5 * (1.0 + jnp.tanh(c * (x + 0.044715 * (x * x * x)))))


def _dot(a, b):
    return jnp.dot(a, b, preferred_element_type=F32)


def _rope_table_kernel(pos_ref, freq_ref, o_ref):
    ang = pos_ref[...] * freq_ref[...]
    lane = lax.broadcasted_iota(jnp.int32, ang.shape, 1)
    o_ref[...] = jnp.where(lane < ROPE_DIM, jnp.cos(ang), jnp.sin(ang))


def _rope_table(pos_f, freq4):
    t = pos_f.shape[0]
    tm = min(ROPE_TM, t)
    return pl.pallas_call(
        _rope_table_kernel,
        grid=(t // tm,),
        in_specs=[pl.BlockSpec((tm, 1), lambda i: (i, 0)),
                  pl.BlockSpec((1, 2 * ROPE_DIM), lambda i: (0, 0))],
        out_specs=pl.BlockSpec((tm, 2 * ROPE_DIM), lambda i: (i, 0)),
        out_shape=jax.ShapeDtypeStruct((t, 2 * ROPE_DIM), F32),
        compiler_params=_params("parallel"),
        name="rope_table",
    )(pos_f, freq4)


def _ffn_kernel(x_ref, gpre_ref, wg_ref, wu_ref, wo_ref, gpost_ref, o_ref, h_ref, acc_ref):
    f = pl.program_id(1)

    @pl.when(f == 0)
    def _():
        h_ref[...] = _rms(x_ref[...], gpre_ref[...]).astype(BF16)
        acc_ref[...] = jnp.zeros_like(acc_ref)

    h = h_ref[...]
    gate = _dot(h, wg_ref[...])
    up = _dot(h, wu_ref[...])
    act = (gate * _sigmoid(gate) * up).astype(BF16)
    acc_ref[...] += _dot(act, wo_ref[...])

    @pl.when(f == pl.num_programs(1) - 1)
    def _():
        o_ref[...] = x_ref[...] + 0.5 * _rms(acc_ref[...], gpost_ref[...])


def _ffn(x, gpre, w_in, w_out, gpost, layer):
    t, d = x.shape
    d_ff = w_out.shape[1]
    tm, tf = min(FFN_TM, t), min(FFN_TF, d_ff)
    nf = d_ff // tf
    return pl.pallas_call(
        _ffn_kernel,
        grid=(t // tm, nf),
        in_specs=[
            pl.BlockSpec((tm, d), lambda i, f: (i, 0)),
            pl.BlockSpec((None, 1, d), lambda i, f: (layer, 0, 0)),
            pl.BlockSpec((None, d, tf), lambda i, f: (layer, 0, f)),
            pl.BlockSpec((None, d, tf), lambda i, f: (layer, 0, f + nf)),
            pl.BlockSpec((None, tf, d), lambda i, f: (layer, f, 0)),
            pl.BlockSpec((None, 1, d), lambda i, f: (layer, 0, 0)),
        ],
        out_specs=pl.BlockSpec((tm, d), lambda i, f: (i, 0)),
        out_shape=jax.ShapeDtypeStruct((t, d), F32),
        scratch_shapes=[pltpu.VMEM((tm, d), BF16), pltpu.VMEM((tm, d), F32)],
        compiler_params=_params("parallel", "arbitrary"),
        name="ffn",
    )(x, gpre, w_in, w_in, w_out, gpost)


def _proj_kernel(x_ref, g_ref, w_ref, o_ref, h_ref):
    @pl.when(pl.program_id(1) == 0)
    def _():
        h_ref[...] = _rms(x_ref[...], g_ref[...]).astype(BF16)

    o_ref[...] = _dot(h_ref[...], w_ref[...]).astype(BF16)


def _proj(x, g, w_ext, layer):
    t, d = x.shape
    n = w_ext.shape[2]
    tm, tn = min(PROJ_TM, t), min(PROJ_TN, n)
    return pl.pallas_call(
        _proj_kernel,
        grid=(t // tm, n // tn),
        in_specs=[
            pl.BlockSpec((tm, d), lambda i, j: (i, 0)),
            pl.BlockSpec((None, 1, d), lambda i, j: (layer, 0, 0)),
            pl.BlockSpec((None, d, tn), lambda i, j: (layer, 0, j)),
        ],
        out_specs=pl.BlockSpec((tm, tn), lambda i, j: (i, j)),
        out_shape=jax.ShapeDtypeStruct((t, n), BF16),
        scratch_shapes=[pltpu.VMEM((tm, d), BF16)],
        compiler_params=_params("parallel", "arbitrary"),
        name="mix_proj",
    )(x, g, w_ext)


def _gmlp_kernel(u_ref, v_ref, lng_ref, lnb_ref, ws_ref, bias_ref, o_ref):
    tm, width = u_ref.shape
    gw = width // GM_GROUPS
    vn = _layer_norm(_gelu_tanh(v_ref[...].astype(F32)), lng_ref[...], lnb_ref[...]).astype(BF16)
    row = lax.broadcasted_iota(jnp.int32, (CHUNK, CHUNK), 0)
    col = lax.broadcasted_iota(jnp.int32, (CHUNK, CHUNK), 1)
    for g in range(GM_GROUPS):
        w = jnp.where(col <= row, ws_ref[g], 0.0).astype(BF16)
        cols = slice(g * gw, (g + 1) * gw)
        for c in range(tm // CHUNK):
            rows = slice(c * CHUNK, (c + 1) * CHUNK)
            s = _dot(w, vn[rows, cols]) + bias_ref[:, cols]
            o_ref[rows, cols] = (_gelu_tanh(u_ref[rows, cols].astype(F32)) * s).astype(BF16)


def _gmlp(p, lng, lnb, w_s, bias_full, layer):
    t = p.shape[0]
    width = lng.shape[2]
    tm = min(GM_TM, t)
    return pl.pallas_call(
        _gmlp_kernel,
        grid=(t // tm,),
        in_specs=[
            pl.BlockSpec((tm, width), lambda i: (i, 0)),
            pl.BlockSpec((tm, width), lambda i: (i, 1)),
            pl.BlockSpec((None, 1, width), lambda i: (layer, 0, 0)),
            pl.BlockSpec((None, 1, width), lambda i: (layer, 0, 0)),
            pl.BlockSpec((None, GM_GROUPS, CHUNK, CHUNK), lambda i: (layer, 0, 0, 0)),
            pl.BlockSpec((None, CHUNK, width), lambda i: (layer, 0, 0)),
        ],
        out_specs=pl.BlockSpec((tm, width), lambda i: (i, 0)),
        out_shape=jax.ShapeDtypeStruct((t, width), BF16),
        compiler_params=_params("parallel"),
        name="gmlp",
    )(p, p, lng, lnb, w_s, bias_full)


def _conv_kernel(val_ref, gate_ref, w_ref, b_ref, lng_ref, lnb_ref, o_ref, ext_ref, sh_ref, y_ref):
    tm, width = val_ref.shape
    i = pl.program_id(1)

    @pl.when(i == 0)
    def _():
        ext_ref[0:CONV_HALO, :] = jnp.zeros((CONV_HALO, width), F32)

    @pl.when(i > 0)
    def _():
        ext_ref[0:CONV_HALO, :] = ext_ref[tm:tm + CONV_HALO, :]

    ext_ref[CONV_HALO:CONV_HALO + tm, :] = val_ref[...].astype(F32) * _sigmoid(gate_ref[...].astype(F32))
    span = sh_ref.shape[1]
    for r in range(1, SUBLANES):
        sh_ref[r - 1] = ext_ref[r:r + span, :]
    first = CONV_HALO - (CONV_K - 1)
    rows = min(CONV_RB, tm)
    for c in range(width // LANES):
        lanes = slice(c * LANES, (c + 1) * LANES)
        for rb in range(tm // rows):
            acc = jnp.broadcast_to(b_ref[:, lanes], (rows, LANES))
            for k in range(CONV_K):
                r = (first + k) % SUBLANES
                row0 = first + k - r + rb * rows
                src = ext_ref if r == 0 else sh_ref.at[r - 1]
                acc = acc + w_ref[k:k + 1, lanes] * src[row0:row0 + rows, lanes]
            y_ref[rb * rows:(rb + 1) * rows, lanes] = acc
    z = _layer_norm(y_ref[...], lng_ref[...], lnb_ref[...])
    o_ref[...] = (z * _sigmoid(z)).astype(BF16)


def _conv(p, w, b, lng, lnb, layer, batch):
    t = p.shape[0]
    width = w.shape[2]
    seq = t // batch
    tm = min(CONV_TM, seq)
    ns = seq // tm
    vblk = 2 * GM_WIDTH // width
    return pl.pallas_call(
        _conv_kernel,
        grid=(batch, ns),
        in_specs=[
            pl.BlockSpec((tm, width), lambda bi, i: (bi * ns + i, vblk)),
            pl.BlockSpec((tm, width), lambda bi, i: (bi * ns + i, vblk + 1)),
            pl.BlockSpec((None, CONV_K, width), lambda bi, i: (layer, 0, 0)),
            pl.BlockSpec((None, 1, width), lambda bi, i: (layer, 0, 0)),
            pl.BlockSpec((None, 1, width), lambda bi, i: (layer, 0, 0)),
            pl.BlockSpec((None, 1, width), lambda bi, i: (layer, 0, 0)),
        ],
        out_specs=pl.BlockSpec((tm, width), lambda bi, i: (bi * ns + i, 0)),
        out_shape=jax.ShapeDtypeStruct((t, width), BF16),
        scratch_shapes=[
            pltpu.VMEM((tm + CONV_HALO, width), F32),
            pltpu.VMEM((SUBLANES - 1, tm + CONV_HALO - SUBLANES, width), F32),
            pltpu.VMEM((tm, width), F32),
        ],
        compiler_params=_params("arbitrary", "arbitrary"),
        name="conv",
    )(p, p, w, b, lng, lnb)


def _rope_pair(pair, cs):
    z = pair * cs
    return z + pltpu.roll(z, ROPE_DIM, axis=1)


def _q_kernel(c_ref, cs_ref, g_ref, w_ref, o_ref, *, q_scale):
    cn = _rms(c_ref[...].astype(F32), g_ref[...]).astype(BF16)
    q = _dot(cn, w_ref[...]) * q_scale
    cs = cs_ref[...]
    for h in range(o_ref.shape[1]):
        base = h * HEAD_QK
        o_ref[0, h, :, 0:NOPE_DIM] = q[:, base:base + NOPE_DIM].astype(BF16)
        rot = _rope_pair(q[:, base + NOPE_DIM:base + HEAD_QK], cs)
        o_ref[0, h, :, NOPE_DIM:HEAD_QK] = rot.astype(BF16)


def _kv_kernel(c_ref, kr_ref, cs_ref, g_ref, wk_ref, wvt_ref, k_ref, vt_ref):
    cn = _rms(c_ref[...].astype(F32), g_ref[...]).astype(BF16)
    kn = _dot(cn, wk_ref[...])
    vt = lax.dot_general(wvt_ref[...], cn, (((1,), (1,)), ((), ())), preferred_element_type=F32)
    rot = _rope_pair(kr_ref[...].astype(F32), cs_ref[...])
    lane = lax.broadcasted_iota(jnp.int32, rot.shape, 1)
    rot = jnp.where(lane < ROPE_DIM, rot, 0.0).astype(BF16)
    for h in range(k_ref.shape[1]):
        k_ref[0, h, :, 0:NOPE_DIM] = kn[:, h * NOPE_DIM:(h + 1) * NOPE_DIM].astype(BF16)
        k_ref[0, h, :, NOPE_DIM:HEAD_QK] = rot
        vt_ref[0, h, 0, 0:V_DIM, :] = vt[h * V_DIM:(h + 1) * V_DIM, :].astype(BF16)
        vt_ref[0, h, 0, V_DIM:V_EXT, :] = jnp.ones((V_EXT - V_DIM, vt.shape[1]), BF16)


def _q_proj(p, cs, g, w_q, layer, batch, col_blk):
    t = p.shape[0]
    rank = w_q.shape[1]
    heads = w_q.shape[2] // HEAD_QK
    seq = t // batch
    tm = min(QKV_TM, seq)
    ns = seq // tm
    q_scale = float((NOPE_DIM + ROPE_DIM) ** -0.5 * math.log2(math.e))
    return pl.pallas_call(
        functools.partial(_q_kernel, q_scale=q_scale),
        grid=(batch, ns),
        in_specs=[
            pl.BlockSpec((tm, rank), lambda bi, i: (bi * ns + i, col_blk)),
            pl.BlockSpec((tm, 2 * ROPE_DIM), lambda bi, i: (bi * ns + i, 0)),
            pl.BlockSpec((None, 1, rank), lambda bi, i: (layer, 0, 0)),
            pl.BlockSpec((None, rank, heads * HEAD_QK), lambda bi, i: (layer, 0, 0)),
        ],
        out_specs=pl.BlockSpec((1, heads, tm, HEAD_QK), lambda bi, i: (bi, 0, i, 0)),
        out_shape=jax.ShapeDtypeStruct((batch, heads, seq, HEAD_QK), BF16),
        compiler_params=_params("parallel", "parallel"),
        name="mla_q",
    )(p, cs, g, w_q)


def _kv_proj(p, cs, g, w_k, w_vt, layer, batch, col_blk, kr_blk):
    t = p.shape[0]
    rank = w_k.shape[1]
    heads = w_k.shape[2] // NOPE_DIM
    seq = t // batch
    tm = min(ATT_T, seq)
    ns = seq // tm
    assert ns % 2 == 0
    return pl.pallas_call(
        _kv_kernel,
        grid=(batch, ns),
        in_specs=[
            pl.BlockSpec((tm, rank), lambda bi, i: (bi * ns + i, col_blk)),
            pl.BlockSpec((tm, 2 * ROPE_DIM), lambda bi, i: (bi * ns + i, kr_blk)),
            pl.BlockSpec((tm, 2 * ROPE_DIM), lambda bi, i: (bi * ns + i, 0)),
            pl.BlockSpec((None, 1, rank), lambda bi, i: (layer, 0, 0)),
            pl.BlockSpec((None, rank, heads * NOPE_DIM), lambda bi, i: (layer, 0, 0)),
            pl.BlockSpec((None, heads * V_DIM, rank), lambda bi, i: (layer, 0, 0)),
        ],
        out_specs=[
            pl.BlockSpec((1, heads, tm, HEAD_QK), lambda bi, i: (bi, 0, i, 0)),
            pl.BlockSpec((1, heads, 1, V_EXT, tm), lambda bi, i: (bi, 0, i // 2, 0, i % 2)),
        ],
        out_shape=[
            jax.ShapeDtypeStruct((batch, heads, seq, HEAD_QK), BF16),
            jax.ShapeDtypeStruct((batch, heads, ns // 2, V_EXT, 2 * tm), BF16),
        ],
        compiler_params=_params("parallel", "parallel"),
        name="mla_kv",
    )(p, p, cs, g, w_k, w_vt)


def _flash_kernel(q_ref, k_ref, vt_ref, o_ref, acc_ref, *, t):
    seq = q_ref.shape[2]

    def scores(q, k):
        return lax.dot_general(k, q, (((1,), (1,)), ((), ())), preferred_element_type=F32)

    def update(chain, s, vt, m, diag_offset):
        if diag_offset is not None:
            key = lax.broadcasted_iota(jnp.int32, s.shape, 0)
            qry = lax.broadcasted_iota(jnp.int32, s.shape, 1)
            s = jnp.where(key <= qry + diag_offset, s, NEG_BIG)
        m_new = jnp.maximum(m, jnp.max(s, axis=0, keepdims=True))
        pr = jnp.exp2(s - m_new).astype(BF16)
        acc_ref[chain] = jnp.exp2(m - m_new) * acc_ref[chain] + _dot(vt, pr)
        return m_new

    def finish(chain, q0):
        o = acc_ref[chain, 0:V_DIM, :] / acc_ref[chain, V_DIM:V_DIM + 1, :]
        o_ref[pl.ds(q0, t), :] = o.T.astype(BF16)

    def pair_body(j, carry):
        qa0 = pl.multiple_of(2 * j * t, 2 * t)
        qb0 = pl.multiple_of(qa0 + t, t)
        qa = q_ref[0, 0, pl.ds(qa0, t), :]
        qb = q_ref[0, 0, pl.ds(qb0, t), :]
        acc_ref[...] = jnp.zeros_like(acc_ref)
        init = jnp.full((1, t), NEG_BIG, F32)

        def full_step(i, ms):
            ma, mb = ms
            k = k_ref[0, 0, pl.ds(pl.multiple_of(i * 2 * t, 2 * t), 2 * t), :]
            vt = vt_ref[0, 0, i]
            halves = [slice(0, t), slice(t, 2 * t)]
            tiles = [(scores(qa, k[h]), scores(qb, k[h]), vt[:, h]) for h in halves]
            for s_a, s_b, vth in tiles:
                ma = update(0, s_a, vth, ma, None)
                mb = update(1, s_b, vth, mb, None)
            return ma, mb

        ma, mb = lax.fori_loop(0, j, full_step, (init, init))
        k = k_ref[0, 0, pl.ds(qa0, 2 * t), :]
        vt = vt_ref[0, 0, j]
        s_a, s_b = scores(qa, k[0:t]), scores(qb, k)
        update(0, s_a, vt[:, 0:t], ma, 0)
        update(1, s_b, vt, mb, t)
        finish(0, qa0)
        finish(1, qb0)
        return carry

    lax.fori_loop(0, seq // (2 * t), pair_body, 0)


def _flash(q, k, vt):
    batch, heads, seq, _ = q.shape
    nk, v_ext, t2 = vt.shape[2:]
    assert nk * t2 == seq
    return pl.pallas_call(
        functools.partial(_flash_kernel, t=t2 // 2),
        grid=(batch, heads),
        in_specs=[
            pl.BlockSpec((1, 1, seq, HEAD_QK), lambda b, h: (b, h, 0, 0)),
            pl.BlockSpec((1, 1, seq, HEAD_QK), lambda b, h: (b, h, 0, 0)),
            pl.BlockSpec((1, 1, nk, v_ext, t2), lambda b, h: (b, h, 0, 0, 0)),
        ],
        out_specs=pl.BlockSpec((seq, V_DIM), lambda b, h: (b, h)),
        out_shape=jax.ShapeDtypeStruct((batch * seq, heads * V_DIM), BF16),
        scratch_shapes=[pltpu.VMEM((2, v_ext, t2 // 2), F32)],
        compiler_params=_params("parallel", "parallel"),
        name="mla_attn",
    )(q, k, vt)


def _merge_kernel(oa_ref, ob_ref, oc_ref, g0_ref, g1_ref, g2_ref, x_ref,
                  wa_ref, wb_ref, wc_ref, wo_ref, gpost_ref, o_ref):
    merged = _sigmoid(g0_ref[...].astype(F32)) * _dot(oa_ref[...], wa_ref[...])
    merged += _sigmoid(g1_ref[...].astype(F32)) * _dot(ob_ref[...], wb_ref[...])
    merged += _sigmoid(g2_ref[...].astype(F32)) * _dot(oc_ref[...], wc_ref[...])
    m = _dot(merged.astype(BF16), wo_ref[...])
    o_ref[...] = x_ref[...] + _rms(m, gpost_ref[...])


def _merge(o_a, o_b, o_c, p, x, w_branch, w_out, gpost, layer):
    t, d = x.shape
    wa, wb, wc = o_a.shape[1], o_b.shape[1], o_c.shape[1]
    tm = min(MERGE_TM, t)
    gblk = (2 * wa + 2 * wb) // d
    once = pl.Buffered(1)
    return pl.pallas_call(
        _merge_kernel,
        grid=(t // tm,),
        in_specs=[
            pl.BlockSpec((tm, wa), lambda i: (i, 0)),
            pl.BlockSpec((tm, wb), lambda i: (i, 0)),
            pl.BlockSpec((tm, wc), lambda i: (i, 0)),
            pl.BlockSpec((tm, d), lambda i: (i, gblk)),
            pl.BlockSpec((tm, d), lambda i: (i, gblk + 1)),
            pl.BlockSpec((tm, d), lambda i: (i, gblk + 2)),
            pl.BlockSpec((tm, d), lambda i: (i, 0)),
            pl.BlockSpec((None, wa, d), lambda i: (layer, 0, 0), pipeline_mode=once),
            pl.BlockSpec((None, wb, d), lambda i: (layer, wa // wb, 0), pipeline_mode=once),
            pl.BlockSpec((None, wc, d), lambda i: (layer, (wa + wb) // wc, 0), pipeline_mode=once),
            pl.BlockSpec((None, d, d), lambda i: (layer, 0, 0), pipeline_mode=once),
            pl.BlockSpec((None, 1, d), lambda i: (layer, 0, 0)),
        ],
        out_specs=pl.BlockSpec((tm, d), lambda i: (i, 0)),
        out_shape=jax.ShapeDtypeStruct((t, d), F32),
        compiler_params=_params("parallel"),
        name="merge",
    )(o_a, o_b, o_c, p, p, p, x, w_branch, w_branch, w_branch, w_out, gpost)


def _rotate_half_columns(w):
    half = w.shape[-1] // 2
    return jnp.concatenate([-w[..., half:], w[..., :half]], axis=-1)


def _mix_in_layout(w_in):
    o = np.cumsum([0, GM_WIDTH, GM_WIDTH, CONV_WIDTH, CONV_WIDTH, Q_RANK, KV_RANK, ROPE_DIM])
    k_rope = w_in[..., o[6]:o[7]]
    body = [w_in[..., :o[4]], w_in[..., o[7]:], w_in[..., o[4]:o[6]], k_rope, _rotate_half_columns(k_rope)]
    n = sum(a.shape[-1] for a in body)
    tn = min(PROJ_TN, n)
    pad = (-n) % tn
    body.append(jnp.zeros(w_in.shape[:-1] + (pad,), w_in.dtype))
    return jnp.concatenate(body, axis=-1).astype(BF16)


def _q_layout(w_uq):
    lead = w_uq.shape[:-1]
    w = w_uq.reshape(lead + (N_HEADS, NOPE_DIM + ROPE_DIM))
    rope = w[..., NOPE_DIM:]
    w = jnp.concatenate([w[..., :NOPE_DIM], rope, _rotate_half_columns(rope)], axis=-1)
    return w.reshape(lead + (N_HEADS * HEAD_QK,)).astype(BF16)


def _kv_layout(w_ukv):
    lead = w_ukv.shape[:-1]
    w = w_ukv.reshape(lead + (N_HEADS, NOPE_DIM + V_DIM))
    w_k = w[..., :NOPE_DIM].reshape(lead + (N_HEADS * NOPE_DIM,))
    w_v = w[..., NOPE_DIM:].reshape(lead + (N_HEADS * V_DIM,))
    return w_k.astype(BF16), jnp.swapaxes(w_v, -1, -2).astype(BF16)


def kernel(x, positions, ffn1_norm_pre, ffn1_norm_post, ffn1_w_in, ffn1_w_out, mix_norm_pre, mix_norm_post, mix_w_in, gm_ln_g, gm_ln_b, gm_w_s, gm_b_s, conv_w, conv_b, conv_ln_g, conv_ln_b, mla_q_norm, mla_w_uq, mla_kv_norm, mla_w_ukv, mix_w_branch, mix_w_out, ffn2_norm_pre, ffn2_norm_post, ffn2_w_in, ffn2_w_out):
    batch, seq, d = x.shape
    depth = ffn1_w_in.shape[0]
    t = batch * seq
    row = lambda a: a[:, None, :]

    inv_freq = ROPE_THETA ** (-jnp.arange(0, ROPE_DIM, 2, dtype=F32) / ROPE_DIM)
    freq4 = jnp.tile(inv_freq, 4)[None, :]
    cs = _rope_table(positions.astype(F32).reshape(t, 1), freq4)

    w_mix = _mix_in_layout(mix_w_in)
    w_q = _q_layout(mla_w_uq)
    w_k, w_vt = _kv_layout(mla_w_ukv)
    w_branch = mix_w_branch.astype(BF16)
    w_out = mix_w_out.astype(BF16)
    f1_in, f1_out = ffn1_w_in.astype(BF16), ffn1_w_out.astype(BF16)
    f2_in, f2_out = ffn2_w_in.astype(BF16), ffn2_w_out.astype(BF16)
    gm_bias = jnp.repeat(jnp.swapaxes(gm_b_s, 1, 2), GM_WIDTH // GM_GROUPS, axis=2)

    gates_end = 2 * GM_WIDTH + 2 * CONV_WIDTH + 3 * d
    cq_blk = gates_end // Q_RANK
    ckv_blk = (gates_end + Q_RANK) // KV_RANK
    kr_blk = (gates_end + Q_RANK + KV_RANK) // (2 * ROPE_DIM)

    xf = x.reshape(t, d)
    for l in range(depth):
        xf = _ffn(xf, row(ffn1_norm_pre), f1_in, f1_out, row(ffn1_norm_post), l)
        p = _proj(xf, row(mix_norm_pre), w_mix, l)
        o_a = _gmlp(p, row(gm_ln_g), row(gm_ln_b), gm_w_s, gm_bias, l)
        o_b = _conv(p, conv_w, row(conv_b), row(conv_ln_g), row(conv_ln_b), l, batch)
        q = _q_proj(p, cs, row(mla_q_norm), w_q, l, batch, cq_blk)
        k, vt = _kv_proj(p, cs, row(mla_kv_norm), w_k, w_vt, l, batch, ckv_blk, kr_blk)
        o_c = _flash(q, k, vt)
        xf = _merge(o_a, o_b, o_c, p, xf, w_branch, w_out, row(mix_norm_post), l)
        xf = _ffn(xf, row(ffn2_norm_pre), f2_in, f2_out, row(ffn2_norm_post), l)
    return xf.reshape(batch, seq, d)
```

```python
import functools
import math

import jax
import jax.numpy as jnp
import numpy as np
from jax import lax
from jax.experimental import pallas as pl
from jax.experimental.pallas import tpu as pltpu

D_MODEL = 2048
D_FF = 4096
GM_WIDTH = 1024
GM_GROUPS = 4
CHUNK = 128
CONV_WIDTH = 1024
CONV_K = 31
N_HEADS = 16
Q_RANK = 512
KV_RANK = 512
NOPE_DIM = 128
ROPE_DIM = 64
V_DIM = 128
ROPE_THETA = 10000.0
EPS = 1e-6

LANES = 128
SUBLANES = 8
CONV_RB = 64
HEAD_QK = NOPE_DIM + 2 * ROPE_DIM
V_EXT = V_DIM + 16
CONV_HALO = 32

FFN_TM = 512
FFN_TF = 1024
PROJ_TM = 1024
PROJ_TN = 1280
GM_TM = 256
ATT_T = 512
MERGE_TM = 256
ROPE_TM = 1024

VMEM_LIMIT = 56 * 1024 * 1024

F32 = jnp.float32
BF16 = jnp.bfloat16
NEG_BIG = -1e30


def _params(*sem):
    return pltpu.CompilerParams(dimension_semantics=sem, vmem_limit_bytes=VMEM_LIMIT)


def _rms(x, g):
    return x * lax.rsqrt(jnp.mean(x * x, axis=-1, keepdims=True) + EPS) * g


def _layer_norm(x, g, b):
    mu = jnp.mean(x, axis=-1, keepdims=True)
    xc = x - mu
    return xc * lax.rsqrt(jnp.mean(xc * xc, axis=-1, keepdims=True) + EPS) * g + b


def _sigmoid(x):
    return 1.0 / (1.0 + jnp.exp(-x))


def _gelu_tanh(x):
    c = math.sqrt(2.0 / math.pi)
    return x * (0.5 * (1.0 + jnp.tanh(c * (x + 0.044715 * (x * x * x)))))


def _dot(a, b):
    return jnp.dot(a, b, preferred_element_type=F32)


def _rope_table_kernel(pos_col_ref, pos_row_ref, freq_row_ref, freq_col_ref, o_ref, ot_ref):
    def table(ang, axis):
        idx = lax.broadcasted_iota(jnp.int32, ang.shape, axis)
        return jnp.where(idx < ROPE_DIM, jnp.cos(ang), jnp.sin(ang))

    o_ref[...] = table(pos_col_ref[...] * freq_row_ref[...], 1)
    ot_ref[...] = table(freq_col_ref[...] * pos_row_ref[...], 0)


def _rope_table(pos_f, freq4):
    t = pos_f.shape[0]
    tm = min(ROPE_TM, t)
    return pl.pallas_call(
        _rope_table_kernel,
        grid=(t // tm,),
        in_specs=[pl.BlockSpec((tm, 1), lambda i: (i, 0)),
                  pl.BlockSpec((1, tm), lambda i: (0, i)),
                  pl.BlockSpec((1, 2 * ROPE_DIM), lambda i: (0, 0)),
                  pl.BlockSpec((2 * ROPE_DIM, 1), lambda i: (0, 0))],
        out_specs=[pl.BlockSpec((tm, 2 * ROPE_DIM), lambda i: (i, 0)),
                   pl.BlockSpec((2 * ROPE_DIM, tm), lambda i: (0, i))],
        out_shape=[jax.ShapeDtypeStruct((t, 2 * ROPE_DIM), F32),
                   jax.ShapeDtypeStruct((2 * ROPE_DIM, t), F32)],
        compiler_params=_params("parallel"),
        name="rope_table",
    )(pos_f.reshape(t, 1), pos_f.reshape(1, t), freq4.reshape(1, -1), freq4.reshape(-1, 1))


def _ffn_kernel(x_ref, gpre_ref, wg_ref, wu_ref, wo_ref, gpost_ref, o_ref, h_ref, acc_ref):
    f = pl.program_id(1)

    @pl.when(f == 0)
    def _():
        h_ref[...] = _rms(x_ref[...], gpre_ref[...]).astype(BF16)
        acc_ref[...] = jnp.zeros_like(acc_ref)

    h = h_ref[...]
    gate = _dot(h, wg_ref[...])
    up = _dot(h, wu_ref[...])
    act = (gate * _sigmoid(gate) * up).astype(BF16)
    acc_ref[...] += _dot(act, wo_ref[...])

    @pl.when(f == pl.num_programs(1) - 1)
    def _():
        o_ref[...] = x_ref[...] + 0.5 * _rms(acc_ref[...], gpost_ref[...])


def _ffn(x, gpre, w_in, w_out, gpost, layer):
    t, d = x.shape
    d_ff = w_out.shape[1]
    tm, tf = min(FFN_TM, t), min(FFN_TF, d_ff)
    nf = d_ff // tf
    return pl.pallas_call(
        _ffn_kernel,
        grid=(t // tm, nf),
        in_specs=[
            pl.BlockSpec((tm, d), lambda i, f: (i, 0)),
            pl.BlockSpec((None, 1, d), lambda i, f: (layer, 0, 0)),
            pl.BlockSpec((None, d, tf), lambda i, f: (layer, 0, f)),
            pl.BlockSpec((None, d, tf), lambda i, f: (layer, 0, f + nf)),
            pl.BlockSpec((None, tf, d), lambda i, f: (layer, f, 0)),
            pl.BlockSpec((None, 1, d), lambda i, f: (layer, 0, 0)),
        ],
        out_specs=pl.BlockSpec((tm, d), lambda i, f: (i, 0)),
        out_shape=jax.ShapeDtypeStruct((t, d), F32),
        scratch_shapes=[pltpu.VMEM((tm, d), BF16), pltpu.VMEM((tm, d), F32)],
        compiler_params=_params("parallel", "arbitrary"),
        name="ffn",
    )(x, gpre, w_in, w_in, w_out, gpost)


def _proj_kernel(x_ref, g_ref, w_ref, o_ref, h_ref):
    @pl.when(pl.program_id(1) == 0)
    def _():
        h_ref[...] = _rms(x_ref[...], g_ref[...]).astype(BF16)

    o_ref[...] = _dot(h_ref[...], w_ref[...]).astype(BF16)


def _proj(x, g, w_ext, layer):
    t, d = x.shape
    n = w_ext.shape[2]
    tm, tn = min(PROJ_TM, t), min(PROJ_TN, n)
    return pl.pallas_call(
        _proj_kernel,
        grid=(t // tm, n // tn),
        in_specs=[
            pl.BlockSpec((tm, d), lambda i, j: (i, 0)),
            pl.BlockSpec((None, 1, d), lambda i, j: (layer, 0, 0)),
            pl.BlockSpec((None, d, tn), lambda i, j: (layer, 0, j)),
        ],
        out_specs=pl.BlockSpec((tm, tn), lambda i, j: (i, j)),
        out_shape=jax.ShapeDtypeStruct((t, n), BF16),
        scratch_shapes=[pltpu.VMEM((tm, d), BF16)],
        compiler_params=_params("parallel", "arbitrary"),
        name="mix_proj",
    )(x, g, w_ext)


def _gmlp_kernel(u_ref, v_ref, lng_ref, lnb_ref, ws_ref, bias_ref, o_ref):
    tm, width = u_ref.shape
    gw = width // GM_GROUPS
    vn = _layer_norm(_gelu_tanh(v_ref[...].astype(F32)), lng_ref[...], lnb_ref[...]).astype(BF16)
    row = lax.broadcasted_iota(jnp.int32, (CHUNK, CHUNK), 0)
    col = lax.broadcasted_iota(jnp.int32, (CHUNK, CHUNK), 1)
    for g in range(GM_GROUPS):
        w = jnp.where(col <= row, ws_ref[g], 0.0).astype(BF16)
        cols = slice(g * gw, (g + 1) * gw)
        for c in range(tm // CHUNK):
            rows = slice(c * CHUNK, (c + 1) * CHUNK)
            s = _dot(w, vn[rows, cols]) + bias_ref[:, cols]
            o_ref[rows, cols] = (_gelu_tanh(u_ref[rows, cols].astype(F32)) * s).astype(BF16)


def _gmlp(p, lng, lnb, w_s, bias_full, layer):
    t = p.shape[0]
    width = lng.shape[2]
    tm = min(GM_TM, t)
    return pl.pallas_call(
        _gmlp_kernel,
        grid=(t // tm,),
        in_specs=[
            pl.BlockSpec((tm, width), lambda i: (i, 0)),
            pl.BlockSpec((tm, width), lambda i: (i, 1)),
            pl.BlockSpec((None, 1, width), lambda i: (layer, 0, 0)),
            pl.BlockSpec((None, 1, width), lambda i: (layer, 0, 0)),
            pl.BlockSpec((None, GM_GROUPS, CHUNK, CHUNK), lambda i: (layer, 0, 0, 0)),
            pl.BlockSpec((None, CHUNK, width), lambda i: (layer, 0, 0)),
        ],
        out_specs=pl.BlockSpec((tm, width), lambda i: (i, 0)),
        out_shape=jax.ShapeDtypeStruct((t, width), BF16),
        compiler_params=_params("parallel"),
        name="gmlp",
    )(p, p, lng, lnb, w_s, bias_full)


def _conv_tile(val_ref, gate_ref, w_ref, b_ref, lng_ref, lnb_ref, ext_ref, sh_ref, y_ref, first_of_seq):
    tm, width = val_ref.shape
    ext_ref[0:CONV_HALO, :] = jnp.where(first_of_seq, 0.0, ext_ref[tm:tm + CONV_HALO, :])
    ext_ref[CONV_HALO:CONV_HALO + tm, :] = val_ref[...].astype(F32) * _sigmoid(gate_ref[...].astype(F32))
    first = CONV_HALO - (CONV_K - 1)
    span = sh_ref.shape[1]
    rows = min(CONV_RB, tm)
    for c in range(width // LANES):
        lanes = slice(c * LANES, (c + 1) * LANES)
        for r in range(1, SUBLANES):
            sh_ref[r - 1] = ext_ref[r:r + span, lanes]
        for rb in range(tm // rows):
            acc = jnp.broadcast_to(b_ref[:, lanes], (rows, LANES))
            for k in range(CONV_K):
                r = (first + k) % SUBLANES
                row0 = first + k - r + rb * rows
                src = ext_ref[row0:row0 + rows, lanes] if r == 0 else sh_ref[r - 1, row0:row0 + rows, :]
                acc = acc + w_ref[k:k + 1, lanes] * src
            y_ref[rb * rows:(rb + 1) * rows, lanes] = acc
    z = _layer_norm(y_ref[...], lng_ref[...], lnb_ref[...])
    return (z * _sigmoid(z)).astype(BF16)


def _rope_pair(pair, cs):
    z = pair * cs
    return z + pltpu.roll(z, ROPE_DIM, axis=1)


def _q_kernel(c_ref, cst_ref, g_ref, wt_ref, o_ref, *, q_scale):
    cn = _rms(c_ref[...].astype(F32), g_ref[...]).astype(BF16)
    qt = lax.dot_general(wt_ref[...], cn, (((1,), (1,)), ((), ())), preferred_element_type=F32) * q_scale
    cst = cst_ref[...]
    for h in range(o_ref.shape[1]):
        base = h * HEAD_QK
        o_ref[0, h, 0, 0:NOPE_DIM, :] = qt[base:base + NOPE_DIM].astype(BF16)
        z = qt[base + NOPE_DIM:base + HEAD_QK] * cst
        rot = (z[0:ROPE_DIM] + z[ROPE_DIM:2 * ROPE_DIM]).astype(BF16)
        o_ref[0, h, 0, NOPE_DIM:NOPE_DIM + ROPE_DIM, :] = rot
        o_ref[0, h, 0, NOPE_DIM + ROPE_DIM:HEAD_QK, :] = jnp.zeros_like(rot)


def _kv_kernel(c_ref, kr_ref, cs_ref, g_ref, wk_ref, wvt_ref, k_ref, vt_ref):
    cn = _rms(c_ref[...].astype(F32), g_ref[...]).astype(BF16)
    kn = _dot(cn, wk_ref[...])
    vt = lax.dot_general(wvt_ref[...], cn, (((1,), (1,)), ((), ())), preferred_element_type=F32)
    rot = _rope_pair(kr_ref[...].astype(F32), cs_ref[...])
    lane = lax.broadcasted_iota(jnp.int32, rot.shape, 1)
    rot = jnp.where(lane < ROPE_DIM, rot, 0.0).astype(BF16)
    for h in range(k_ref.shape[1]):
        k_ref[0, h, :, 0:NOPE_DIM] = kn[:, h * NOPE_DIM:(h + 1) * NOPE_DIM].astype(BF16)
        k_ref[0, h, :, NOPE_DIM:HEAD_QK] = rot
        vt_ref[0, h, 0, 0:V_DIM, :] = vt[h * V_DIM:(h + 1) * V_DIM, :].astype(BF16)
        vt_ref[0, h, 0, V_DIM:V_EXT, :] = jnp.ones((V_EXT - V_DIM, vt.shape[1]), BF16)


def _q_proj(p, cst, g, w_qt, layer, batch, col_blk):
    t = p.shape[0]
    rank = w_qt.shape[2]
    heads = w_qt.shape[1] // HEAD_QK
    seq = t // batch
    tm = min(ATT_T, seq)
    ns = seq // tm
    q_scale = float((NOPE_DIM + ROPE_DIM) ** -0.5 * math.log2(math.e))
    return pl.pallas_call(
        functools.partial(_q_kernel, q_scale=q_scale),
        grid=(batch, ns),
        in_specs=[
            pl.BlockSpec((tm, rank), lambda bi, i: (bi * ns + i, col_blk)),
            pl.BlockSpec((2 * ROPE_DIM, tm), lambda bi, i: (0, bi * ns + i)),
            pl.BlockSpec((None, 1, rank), lambda bi, i: (layer, 0, 0)),
            pl.BlockSpec((None, heads * HEAD_QK, rank), lambda bi, i: (layer, 0, 0)),
        ],
        out_specs=pl.BlockSpec((1, heads, 1, HEAD_QK, tm), lambda bi, i: (bi, 0, i, 0, 0)),
        out_shape=jax.ShapeDtypeStruct((batch, heads, ns, HEAD_QK, tm), BF16),
        compiler_params=_params("parallel", "parallel"),
        name="mla_q",
    )(p, cst, g, w_qt)


def _kv_proj(p, cs, g, w_k, w_vt, layer, batch, col_blk, kr_blk):
    t = p.shape[0]
    rank = w_k.shape[1]
    heads = w_k.shape[2] // NOPE_DIM
    seq = t // batch
    tm = min(ATT_T, seq)
    ns = seq // tm
    assert ns % 2 == 0
    return pl.pallas_call(
        _kv_kernel,
        grid=(batch, ns),
        in_specs=[
            pl.BlockSpec((tm, rank), lambda bi, i: (bi * ns + i, col_blk)),
            pl.BlockSpec((tm, 2 * ROPE_DIM), lambda bi, i: (bi * ns + i, kr_blk)),
            pl.BlockSpec((tm, 2 * ROPE_DIM), lambda bi, i: (bi * ns + i, 0)),
            pl.BlockSpec((None, 1, rank), lambda bi, i: (layer, 0, 0)),
            pl.BlockSpec((None, rank, heads * NOPE_DIM), lambda bi, i: (layer, 0, 0)),
            pl.BlockSpec((None, heads * V_DIM, rank), lambda bi, i: (layer, 0, 0)),
        ],
        out_specs=[
            pl.BlockSpec((1, heads, tm, HEAD_QK), lambda bi, i: (bi, 0, i, 0)),
            pl.BlockSpec((1, heads, 1, V_EXT, tm), lambda bi, i: (bi, 0, i // 2, 0, i % 2)),
        ],
        out_shape=[
            jax.ShapeDtypeStruct((batch, heads, seq, HEAD_QK), BF16),
            jax.ShapeDtypeStruct((batch, heads, ns // 2, V_EXT, 2 * tm), BF16),
        ],
        compiler_params=_params("parallel", "parallel"),
        name="mla_kv",
    )(p, p, cs, g, w_k, w_vt)


def _flash_kernel(qt_ref, k_ref, vt_ref, o_ref, acc_ref, *, t):
    seq = k_ref.shape[2]

    def scores(qt, k):
        return _dot(k, qt)

    def update(chain, s, vt, m, diag_offset):
        if diag_offset is not None:
            key = lax.broadcasted_iota(jnp.int32, s.shape, 0)
            qry = lax.broadcasted_iota(jnp.int32, s.shape, 1)
            s = jnp.where(key <= qry + diag_offset, s, NEG_BIG)
        m_new = jnp.maximum(m, jnp.max(s, axis=0, keepdims=True))
        pr = jnp.exp2(s - m_new).astype(BF16)
        acc_ref[chain] = jnp.exp2(m - m_new) * acc_ref[chain] + _dot(vt, pr)
        return m_new

    def finish(chain, q0):
        o = acc_ref[chain, 0:V_DIM, :] / acc_ref[chain, V_DIM:V_DIM + 1, :]
        o_ref[pl.ds(q0, t), :] = o.T.astype(BF16)

    def pair_body(j, carry):
        qa0 = pl.multiple_of(2 * j * t, 2 * t)
        qb0 = pl.multiple_of(qa0 + t, t)
        qa = qt_ref[0, 0, 2 * j]
        qb = qt_ref[0, 0, 2 * j + 1]
        acc_ref[...] = jnp.zeros_like(acc_ref)
        init = jnp.full((1, t), NEG_BIG, F32)

        def full_step(i, ms):
            ma, mb = ms
            k = k_ref[0, 0, pl.ds(pl.multiple_of(i * 2 * t, 2 * t), 2 * t), :]
            vt = vt_ref[0, 0, i]
            halves = [slice(0, t), slice(t, 2 * t)]
            tiles = [(scores(qa, k[h]), scores(qb, k[h]), vt[:, h]) for h in halves]
            for s_a, s_b, vth in tiles:
                ma = update(0, s_a, vth, ma, None)
                mb = update(1, s_b, vth, mb, None)
            return ma, mb

        ma, mb = lax.fori_loop(0, j, full_step, (init, init))
        k = k_ref[0, 0, pl.ds(qa0, 2 * t), :]
        vt = vt_ref[0, 0, j]
        s_a, s_b = scores(qa, k[0:t]), scores(qb, k)
        update(0, s_a, vt[:, 0:t], ma, 0)
        update(1, s_b, vt, mb, t)
        finish(0, qa0)
        finish(1, qb0)
        return carry

    lax.fori_loop(0, seq // (2 * t), pair_body, 0)


def _flash(qt, k, vt):
    batch, heads, seq, _ = k.shape
    nk, v_ext, t2 = vt.shape[2:]
    nq, _, t = qt.shape[2:]
    assert nk * t2 == seq and nq * t == seq and t2 == 2 * t
    return pl.pallas_call(
        functools.partial(_flash_kernel, t=t),
        grid=(batch, heads),
        in_specs=[
            pl.BlockSpec((1, 1, nq, HEAD_QK, t), lambda b, h: (b, h, 0, 0, 0)),
            pl.BlockSpec((1, 1, seq, HEAD_QK), lambda b, h: (b, h, 0, 0)),
            pl.BlockSpec((1, 1, nk, v_ext, t2), lambda b, h: (b, h, 0, 0, 0)),
        ],
        out_specs=pl.BlockSpec((seq, V_DIM), lambda b, h: (b, h)),
        out_shape=jax.ShapeDtypeStruct((batch * seq, heads * V_DIM), BF16),
        scratch_shapes=[pltpu.VMEM((2, v_ext, t2 // 2), F32)],
        compiler_params=_params("parallel", "parallel"),
        name="mla_attn",
    )(qt, k, vt)


def _merge_kernel(val_ref, gate_ref, oa_ref, oc_ref, g0_ref, g1_ref, g2_ref, x_ref,
                  cw_ref, cb_ref, clng_ref, clnb_ref, wa_ref, wb_ref, wc_ref, wo_ref, gpost_ref,
                  o_ref, ob_ref, ext_ref, sh_ref, y_ref, *, tiles_per_seq):
    s = pl.program_id(0)

    @pl.when(s == 0)
    def _():
        ob_ref[...] = jnp.zeros_like(ob_ref)
        ext_ref[...] = jnp.zeros_like(ext_ref)

    merged = _sigmoid(g0_ref[...].astype(F32)) * _dot(oa_ref[...], wa_ref[...])
    merged += _sigmoid(g1_ref[...].astype(F32)) * _dot(ob_ref[(s + 1) % 2], wb_ref[...])
    merged += _sigmoid(g2_ref[...].astype(F32)) * _dot(oc_ref[...], wc_ref[...])
    m = _dot(merged.astype(BF16), wo_ref[...])
    o_ref[...] = x_ref[...] + _rms(m, gpost_ref[...])

    ob_ref[s % 2] = _conv_tile(val_ref, gate_ref, cw_ref, cb_ref, clng_ref, clnb_ref,
                               ext_ref, sh_ref, y_ref, s % tiles_per_seq == 0)


def _merge(o_a, o_c, p, x, conv_w, conv_b, conv_lng, conv_lnb, w_branch, w_out, gpost, layer, batch):
    t, d = x.shape
    wa, wb, wc = o_a.shape[1], conv_w.shape[2], o_c.shape[1]
    tm = min(MERGE_TM, t // batch)
    n = t // tm
    vblk = 2 * wa // wb
    gblk = (2 * wa + 2 * wb) // d
    once = pl.Buffered(1)
    ahead = lambda s: jnp.minimum(s, n - 1)
    cur = lambda s: jnp.maximum(s - 1, 0)
    vec = lambda w: pl.BlockSpec((None, 1, w), lambda s: (layer, 0, 0))
    return pl.pallas_call(
        functools.partial(_merge_kernel, tiles_per_seq=t // batch // tm),
        grid=(n + 1,),
        in_specs=[
            pl.BlockSpec((tm, wb), lambda s: (ahead(s), vblk)),
            pl.BlockSpec((tm, wb), lambda s: (ahead(s), vblk + 1)),
            pl.BlockSpec((tm, wa), lambda s: (cur(s), 0)),
            pl.BlockSpec((tm, wc), lambda s: (cur(s), 0)),
            pl.BlockSpec((tm, d), lambda s: (cur(s), gblk)),
            pl.BlockSpec((tm, d), lambda s: (cur(s), gblk + 1)),
            pl.BlockSpec((tm, d), lambda s: (cur(s), gblk + 2)),
            pl.BlockSpec((tm, d), lambda s: (cur(s), 0)),
            pl.BlockSpec((None, CONV_K, wb), lambda s: (layer, 0, 0)),
            vec(wb), vec(wb), vec(wb),
            pl.BlockSpec((None, wa, d), lambda s: (layer, 0, 0), pipeline_mode=once),
            pl.BlockSpec((None, wb, d), lambda s: (layer, wa // wb, 0), pipeline_mode=once),
            pl.BlockSpec((None, wc, d), lambda s: (layer, (wa + wb) // wc, 0), pipeline_mode=once),
            pl.BlockSpec((None, d, d), lambda s: (layer, 0, 0), pipeline_mode=once),
            vec(d),
        ],
        out_specs=pl.BlockSpec((tm, d), lambda s: (cur(s), 0)),
        out_shape=jax.ShapeDtypeStruct((t, d), F32),
        scratch_shapes=[
            pltpu.VMEM((2, tm, wb), BF16),
            pltpu.VMEM((tm + CONV_HALO, wb), F32),
            pltpu.VMEM((SUBLANES - 1, tm + CONV_HALO - SUBLANES, LANES), F32),
            pltpu.VMEM((tm, wb), F32),
        ],
        compiler_params=_params("arbitrary"),
        name="merge",
    )(p, p, o_a, o_c, p, p, p, x, conv_w, conv_b, conv_lng, conv_lnb,
      w_branch, w_branch, w_branch, w_out, gpost)


def _rotate_half_columns(w):
    half = w.shape[-1] // 2
    return jnp.concatenate([-w[..., half:], w[..., :half]], axis=-1)


def _mix_in_layout(w_in):
    o = np.cumsum([0, GM_WIDTH, GM_WIDTH, CONV_WIDTH, CONV_WIDTH, Q_RANK, KV_RANK, ROPE_DIM])
    w_in = w_in.astype(BF16)
    k_rope = w_in[..., o[6]:o[7]]
    body = [w_in[..., :o[4]], w_in[..., o[7]:], w_in[..., o[4]:o[6]], k_rope, _rotate_half_columns(k_rope)]
    n = sum(a.shape[-1] for a in body)
    tn = min(PROJ_TN, n)
    pad = (-n) % tn
    body.append(jnp.zeros(w_in.shape[:-1] + (pad,), w_in.dtype))
    return jnp.concatenate(body, axis=-1).astype(BF16)


def _q_layout(w_uq):
    lead = w_uq.shape[:-1]
    w = w_uq.reshape(lead + (N_HEADS, NOPE_DIM + ROPE_DIM))
    rope = w[..., NOPE_DIM:]
    w = jnp.concatenate([w[..., :NOPE_DIM], rope, _rotate_half_columns(rope)], axis=-1)
    w = w.reshape(lead + (N_HEADS * HEAD_QK,))
    return jnp.swapaxes(w, -1, -2).astype(BF16)


def _kv_layout(w_ukv):
    lead = w_ukv.shape[:-1]
    w = w_ukv.reshape(lead + (N_HEADS, NOPE_DIM + V_DIM))
    w_k = w[..., :NOPE_DIM].reshape(lead + (N_HEADS * NOPE_DIM,))
    w_v = w[..., NOPE_DIM:].reshape(lead + (N_HEADS * V_DIM,))
    return w_k.astype(BF16), jnp.swapaxes(w_v, -1, -2).astype(BF16)


def kernel(x, positions, ffn1_norm_pre, ffn1_norm_post, ffn1_w_in, ffn1_w_out, mix_norm_pre, mix_norm_post, mix_w_in, gm_ln_g, gm_ln_b, gm_w_s, gm_b_s, conv_w, conv_b, conv_ln_g, conv_ln_b, mla_q_norm, mla_w_uq, mla_kv_norm, mla_w_ukv, mix_w_branch, mix_w_out, ffn2_norm_pre, ffn2_norm_post, ffn2_w_in, ffn2_w_out):
    batch, seq, d = x.shape
    depth = ffn1_w_in.shape[0]
    t = batch * seq
    row = lambda a: a[:, None, :]

    inv_freq = ROPE_THETA ** (-jnp.arange(0, ROPE_DIM, 2, dtype=F32) / ROPE_DIM)
    freq4 = jnp.tile(inv_freq, 4)[None, :]
    cs, cst = _rope_table(positions.astype(F32).reshape(t), freq4)

    w_mix = _mix_in_layout(mix_w_in)
    w_qt = _q_layout(mla_w_uq)
    w_k, w_vt = _kv_layout(mla_w_ukv)
    w_branch = mix_w_branch.astype(BF16)
    w_out = mix_w_out.astype(BF16)
    f1_in, f1_out = ffn1_w_in.astype(BF16), ffn1_w_out.astype(BF16)
    f2_in, f2_out = ffn2_w_in.astype(BF16), ffn2_w_out.astype(BF16)
    gm_bias = jnp.repeat(jnp.swapaxes(gm_b_s, 1, 2), GM_WIDTH // GM_GROUPS, axis=2)

    gates_end = 2 * GM_WIDTH + 2 * CONV_WIDTH + 3 * d
    cq_blk = gates_end // Q_RANK
    ckv_blk = (gates_end + Q_RANK) // KV_RANK
    kr_blk = (gates_end + Q_RANK + KV_RANK) // (2 * ROPE_DIM)

    xf = x.reshape(t, d)
    for l in range(depth):
        xf = _ffn(xf, row(ffn1_norm_pre), f1_in, f1_out, row(ffn1_norm_post), l)
        p = _proj(xf, row(mix_norm_pre), w_mix, l)
        o_a = _gmlp(p, row(gm_ln_g), row(gm_ln_b), gm_w_s, gm_bias, l)
        qt = _q_proj(p, cst, row(mla_q_norm), w_qt, l, batch, cq_blk)
        k, vt = _kv_proj(p, cs, row(mla_kv_norm), w_k, w_vt, l, batch, ckv_blk, kr_blk)
        o_c = _flash(qt, k, vt)
        xf = _merge(o_a, o_c, p, xf, conv_w, row(conv_b), row(conv_ln_g), row(conv_ln_b),
                    w_branch, w_out, row(mix_norm_post), l, batch)
        xf = _ffn(xf, row(ffn2_norm_pre), f2_in, f2_out, row(ffn2_norm_post), l)
    return xf.reshape(batch, seq, d)
```

```python
import functools
import math

import jax
import jax.numpy as jnp
import numpy as np
from jax import lax
from jax.experimental import pallas as pl
from jax.experimental.pallas import tpu as pltpu

D_MODEL = 2048
D_FF = 4096
GM_WIDTH = 1024
GM_GROUPS = 4
CHUNK = 128
CONV_WIDTH = 1024
CONV_K = 31
N_HEADS = 16
Q_RANK = 512
KV_RANK = 512
NOPE_DIM = 128
ROPE_DIM = 64
V_DIM = 128
ROPE_THETA = 10000.0
EPS = 1e-6

LANES = 128
SUBLANES = 8
CONV_RB = 64
HEAD_QK = NOPE_DIM + 2 * ROPE_DIM
V_EXT = V_DIM + 16
CONV_HALO = 32

FFN_TM = 512
FFN_TF = 1024
PROJ_TM = 1024
PROJ_TN = 1280
GM_TM = 256
ATT_T = 512
MERGE_TM = 256
ROPE_TM = 1024

VMEM_LIMIT = 56 * 1024 * 1024

F32 = jnp.float32
BF16 = jnp.bfloat16
NEG_BIG = -1e30


def _params(*sem):
    return pltpu.CompilerParams(dimension_semantics=sem, vmem_limit_bytes=VMEM_LIMIT)


def _rms(x, g):
    return x * lax.rsqrt(jnp.mean(x * x, axis=-1, keepdims=True) + EPS) * g


def _layer_norm(x, g, b):
    mu = jnp.mean(x, axis=-1, keepdims=True)
    xc = x - mu
    return xc * lax.rsqrt(jnp.mean(xc * xc, axis=-1, keepdims=True) + EPS) * g + b


def _sigmoid(x):
    return 1.0 / (1.0 + jnp.exp(-x))


def _gelu_tanh(x):
    c = math.sqrt(2.0 / math.pi)
    return x * (0.5 * (1.0 + jnp.tanh(c * (x + 0.044715 * (x * x * x)))))


def _dot(a, b):
    return jnp.dot(a, b, preferred_element_type=F32)


def _rope_table_kernel(pos_col_ref, pos_row_ref, freq_row_ref, freq_col_ref, o_ref, ot_ref):
    def table(ang, axis):
        idx = lax.broadcasted_iota(jnp.int32, ang.shape, axis)
        return jnp.where(idx < ROPE_DIM, jnp.cos(ang), jnp.sin(ang))

    o_ref[...] = table(pos_col_ref[...] * freq_row_ref[...], 1)
    ot_ref[...] = table(freq_col_ref[...] * pos_row_ref[...], 0)


def _rope_table(pos_f, freq4):
    t = pos_f.shape[0]
    tm = min(ROPE_TM, t)
    return pl.pallas_call(
        _rope_table_kernel,
        grid=(t // tm,),
        in_specs=[pl.BlockSpec((tm, 1), lambda i: (i, 0)),
                  pl.BlockSpec((1, tm), lambda i: (0, i)),
                  pl.BlockSpec((1, 2 * ROPE_DIM), lambda i: (0, 0)),
                  pl.BlockSpec((2 * ROPE_DIM, 1), lambda i: (0, 0))],
        out_specs=[pl.BlockSpec((tm, 2 * ROPE_DIM), lambda i: (i, 0)),
                   pl.BlockSpec((2 * ROPE_DIM, tm), lambda i: (0, i))],
        out_shape=[jax.ShapeDtypeStruct((t, 2 * ROPE_DIM), F32),
                   jax.ShapeDtypeStruct((2 * ROPE_DIM, t), F32)],
        compiler_params=_params("parallel"),
        name="rope_table",
    )(pos_f.reshape(t, 1), pos_f.reshape(1, t), freq4.reshape(1, -1), freq4.reshape(-1, 1))


def _ffn_kernel(x_ref, gpre_ref, wg_ref, wu_ref, wo_ref, gpost_ref, o_ref, h_ref, acc_ref):
    f = pl.program_id(1)

    @pl.when(f == 0)
    def _():
        h_ref[...] = _rms(x_ref[...], gpre_ref[...]).astype(BF16)
        acc_ref[...] = jnp.zeros_like(acc_ref)

    h = h_ref[...]
    gate = _dot(h, wg_ref[...])
    up = _dot(h, wu_ref[...])
    act = (gate * _sigmoid(gate) * up).astype(BF16)
    acc_ref[...] += _dot(act, wo_ref[...])

    @pl.when(f == pl.num_programs(1) - 1)
    def _():
        o_ref[...] = x_ref[...] + 0.5 * _rms(acc_ref[...], gpost_ref[...])


def _ffn(x, gpre, w_in, w_out, gpost, layer):
    t, d = x.shape
    d_ff = w_out.shape[1]
    tm, tf = min(FFN_TM, t), min(FFN_TF, d_ff)
    nf = d_ff // tf
    return pl.pallas_call(
        _ffn_kernel,
        grid=(t // tm, nf),
        in_specs=[
            pl.BlockSpec((tm, d), lambda i, f: (i, 0)),
            pl.BlockSpec((None, 1, d), lambda i, f: (layer, 0, 0)),
            pl.BlockSpec((None, d, tf), lambda i, f: (layer, 0, f)),
            pl.BlockSpec((None, d, tf), lambda i, f: (layer, 0, f + nf)),
            pl.BlockSpec((None, tf, d), lambda i, f: (layer, f, 0)),
            pl.BlockSpec((None, 1, d), lambda i, f: (layer, 0, 0)),
        ],
        out_specs=pl.BlockSpec((tm, d), lambda i, f: (i, 0)),
        out_shape=jax.ShapeDtypeStruct((t, d), F32),
        scratch_shapes=[pltpu.VMEM((tm, d), BF16), pltpu.VMEM((tm, d), F32)],
        compiler_params=_params("parallel", "arbitrary"),
        name="ffn",
    )(x, gpre, w_in, w_in, w_out, gpost)


def _proj_kernel(x_ref, g_ref, w_ref, o_ref, h_ref):
    @pl.when(pl.program_id(1) == 0)
    def _():
        h_ref[...] = _rms(x_ref[...], g_ref[...]).astype(BF16)

    o_ref[...] = _dot(h_ref[...], w_ref[...]).astype(BF16)


def _proj(x, g, w_ext, layer):
    t, d = x.shape
    n = w_ext.shape[2]
    tm, tn = min(PROJ_TM, t), min(PROJ_TN, n)
    return pl.pallas_call(
        _proj_kernel,
        grid=(t // tm, n // tn),
        in_specs=[
            pl.BlockSpec((tm, d), lambda i, j: (i, 0)),
            pl.BlockSpec((None, 1, d), lambda i, j: (layer, 0, 0)),
            pl.BlockSpec((None, d, tn), lambda i, j: (layer, 0, j)),
        ],
        out_specs=pl.BlockSpec((tm, tn), lambda i, j: (i, j)),
        out_shape=jax.ShapeDtypeStruct((t, n), BF16),
        scratch_shapes=[pltpu.VMEM((tm, d), BF16)],
        compiler_params=_params("parallel", "arbitrary"),
        name="mix_proj",
    )(x, g, w_ext)


def _gmlp_kernel(u_ref, v_ref, lng_ref, lnb_ref, ws_ref, bias_ref, o_ref):
    tm, width = u_ref.shape
    gw = width // GM_GROUPS
    vn = _layer_norm(_gelu_tanh(v_ref[...].astype(F32)), lng_ref[...], lnb_ref[...]).astype(BF16)
    row = lax.broadcasted_iota(jnp.int32, (CHUNK, CHUNK), 0)
    col = lax.broadcasted_iota(jnp.int32, (CHUNK, CHUNK), 1)
    for g in range(GM_GROUPS):
        w = jnp.where(col <= row, ws_ref[g], 0.0).astype(BF16)
        cols = slice(g * gw, (g + 1) * gw)
        for c in range(tm // CHUNK):
            rows = slice(c * CHUNK, (c + 1) * CHUNK)
            s = _dot(w, vn[rows, cols]) + bias_ref[:, cols]
            o_ref[rows, cols] = (_gelu_tanh(u_ref[rows, cols].astype(F32)) * s).astype(BF16)


def _gmlp(p, lng, lnb, w_s, bias_full, layer):
    t = p.shape[0]
    width = lng.shape[2]
    tm = min(GM_TM, t)
    return pl.pallas_call(
        _gmlp_kernel,
        grid=(t // tm,),
        in_specs=[
            pl.BlockSpec((tm, width), lambda i: (i, 0)),
            pl.BlockSpec((tm, width), lambda i: (i, 1)),
            pl.BlockSpec((None, 1, width), lambda i: (layer, 0, 0)),
            pl.BlockSpec((None, 1, width), lambda i: (layer, 0, 0)),
            pl.BlockSpec((None, GM_GROUPS, CHUNK, CHUNK), lambda i: (layer, 0, 0, 0)),
            pl.BlockSpec((None, CHUNK, width), lambda i: (layer, 0, 0)),
        ],
        out_specs=pl.BlockSpec((tm, width), lambda i: (i, 0)),
        out_shape=jax.ShapeDtypeStruct((t, width), BF16),
        compiler_params=_params("parallel"),
        name="gmlp",
    )(p, p, lng, lnb, w_s, bias_full)


def _conv_tile(val_ref, gate_ref, w_ref, b_ref, lng_ref, lnb_ref, ext_ref, sh_ref, y_ref, first_of_seq):
    tm, width = val_ref.shape
    ext_ref[0:CONV_HALO, :] = jnp.where(first_of_seq, 0.0, ext_ref[tm:tm + CONV_HALO, :])
    ext_ref[CONV_HALO:CONV_HALO + tm, :] = val_ref[...].astype(F32) * _sigmoid(gate_ref[...].astype(F32))
    first = CONV_HALO - (CONV_K - 1)
    span = sh_ref.shape[1]
    rows = min(CONV_RB, tm)
    for c in range(width // LANES):
        lanes = slice(c * LANES, (c + 1) * LANES)
        for r in range(1, SUBLANES):
            sh_ref[r - 1] = ext_ref[r:r + span, lanes]
        for rb in range(tm // rows):
            acc = jnp.broadcast_to(b_ref[:, lanes], (rows, LANES))
            for k in range(CONV_K):
                r = (first + k) % SUBLANES
                row0 = first + k - r + rb * rows
                src = ext_ref[row0:row0 + rows, lanes] if r == 0 else sh_ref[r - 1, row0:row0 + rows, :]
                acc = acc + w_ref[k:k + 1, lanes] * src
            y_ref[rb * rows:(rb + 1) * rows, lanes] = acc
    z = _layer_norm(y_ref[...], lng_ref[...], lnb_ref[...])
    return (z * _sigmoid(z)).astype(BF16)


def _rope_pair(pair, cs):
    z = pair * cs
    return z + pltpu.roll(z, ROPE_DIM, axis=1)


def _q_kernel(c_ref, cst_ref, g_ref, wt_ref, o_ref, *, q_scale):
    cn = _rms(c_ref[...].astype(F32), g_ref[...]).astype(BF16)
    qt = lax.dot_general(wt_ref[...], cn, (((1,), (1,)), ((), ())), preferred_element_type=F32) * q_scale
    cst = cst_ref[...]
    for h in range(o_ref.shape[1]):
        base = h * HEAD_QK
        o_ref[0, h, 0, 0:NOPE_DIM, :] = qt[base:base + NOPE_DIM].astype(BF16)
        z = qt[base + NOPE_DIM:base + HEAD_QK] * cst
        rot = (z[0:ROPE_DIM] + z[ROPE_DIM:2 * ROPE_DIM]).astype(BF16)
        o_ref[0, h, 0, NOPE_DIM:NOPE_DIM + ROPE_DIM, :] = rot
        o_ref[0, h, 0, NOPE_DIM + ROPE_DIM:HEAD_QK, :] = jnp.zeros_like(rot)


def _kv_kernel(c_ref, kr_ref, cs_ref, g_ref, wk_ref, wvt_ref, k_ref, vt_ref):
    cn = _rms(c_ref[...].astype(F32), g_ref[...]).astype(BF16)
    kn = _dot(cn, wk_ref[...])
    vt = lax.dot_general(wvt_ref[...], cn, (((1,), (1,)), ((), ())), preferred_element_type=F32)
    rot = _rope_pair(kr_ref[...].astype(F32), cs_ref[...])
    lane = lax.broadcasted_iota(jnp.int32, rot.shape, 1)
    rot = jnp.where(lane < ROPE_DIM, rot, 0.0).astype(BF16)
    for h in range(k_ref.shape[1]):
        k_ref[0, h, :, 0:NOPE_DIM] = kn[:, h * NOPE_DIM:(h + 1) * NOPE_DIM].astype(BF16)
        k_ref[0, h, :, NOPE_DIM:HEAD_QK] = rot
        vt_ref[0, h, 0, 0:V_DIM, :] = vt[h * V_DIM:(h + 1) * V_DIM, :].astype(BF16)
        vt_ref[0, h, 0, V_DIM:V_EXT, :] = jnp.ones((V_EXT - V_DIM, vt.shape[1]), BF16)


def _q_proj(p, cst, g, w_qt, layer, batch, col_blk):
    t = p.shape[0]
    rank = w_qt.shape[2]
    heads = w_qt.shape[1] // HEAD_QK
    seq = t // batch
    tm = min(ATT_T, seq)
    ns = seq // tm
    q_scale = float((NOPE_DIM + ROPE_DIM) ** -0.5 * math.log2(math.e))
    return pl.pallas_call(
        functools.partial(_q_kernel, q_scale=q_scale),
        grid=(batch, ns),
        in_specs=[
            pl.BlockSpec((tm, rank), lambda bi, i: (bi * ns + i, col_blk)),
            pl.BlockSpec((2 * ROPE_DIM, tm), lambda bi, i: (0, bi * ns + i)),
            pl.BlockSpec((None, 1, rank), lambda bi, i: (layer, 0, 0)),
            pl.BlockSpec((None, heads * HEAD_QK, rank), lambda bi, i: (layer, 0, 0)),
        ],
        out_specs=pl.BlockSpec((1, heads, 1, HEAD_QK, tm), lambda bi, i: (bi, 0, i, 0, 0)),
        out_shape=jax.ShapeDtypeStruct((batch, heads, ns, HEAD_QK, tm), BF16),
        compiler_params=_params("parallel", "parallel"),
        name="mla_q",
    )(p, cst, g, w_qt)


def _kv_proj(p, cs, g, w_k, w_vt, layer, batch, col_blk, kr_blk):
    t = p.shape[0]
    rank = w_k.shape[1]
    heads = w_k.shape[2] // NOPE_DIM
    seq = t // batch
    tm = min(ATT_T, seq)
    ns = seq // tm
    assert ns % 2 == 0
    return pl.pallas_call(
        _kv_kernel,
        grid=(batch, ns),
        in_specs=[
            pl.BlockSpec((tm, rank), lambda bi, i: (bi * ns + i, col_blk)),
            pl.BlockSpec((tm, 2 * ROPE_DIM), lambda bi, i: (bi * ns + i, kr_blk)),
            pl.BlockSpec((tm, 2 * ROPE_DIM), lambda bi, i: (bi * ns + i, 0)),
            pl.BlockSpec((None, 1, rank), lambda bi, i: (layer, 0, 0)),
            pl.BlockSpec((None, rank, heads * NOPE_DIM), lambda bi, i: (layer, 0, 0)),
            pl.BlockSpec((None, heads * V_DIM, rank), lambda bi, i: (layer, 0, 0)),
        ],
        out_specs=[
            pl.BlockSpec((1, heads, tm, HEAD_QK), lambda bi, i: (bi, 0, i, 0)),
            pl.BlockSpec((1, heads, 1, V_EXT, tm), lambda bi, i: (bi, 0, i // 2, 0, i % 2)),
        ],
        out_shape=[
            jax.ShapeDtypeStruct((batch, heads, seq, HEAD_QK), BF16),
            jax.ShapeDtypeStruct((batch, heads, ns // 2, V_EXT, 2 * tm), BF16),
        ],
        compiler_params=_params("parallel", "parallel"),
        name="mla_kv",
    )(p, p, cs, g, w_k, w_vt)


def _flash_kernel(qt_ref, k_ref, vt_ref, o_ref, acc_ref, s_ref, *, t):
    seq = k_ref.shape[2]

    def scores(qt, k):
        return _dot(k, qt)

    def update(chain, s, vt, m, diag_offset):
        if diag_offset is not None:
            key = lax.broadcasted_iota(jnp.int32, s.shape, 0)
            qry = lax.broadcasted_iota(jnp.int32, s.shape, 1)
            s = jnp.where(key <= qry + diag_offset, s, NEG_BIG)
        m_new = jnp.maximum(m, jnp.max(s, axis=0, keepdims=True))
        pr = jnp.exp2(s - m_new).astype(BF16)
        acc_ref[chain] = jnp.exp2(m - m_new) * acc_ref[chain] + _dot(vt, pr)
        return m_new

    def finish(chain, q0):
        o = acc_ref[chain, 0:V_DIM, :] / acc_ref[chain, V_DIM:V_DIM + 1, :]
        o_ref[pl.ds(q0, t), :] = o.T.astype(BF16)

    n_pairs = seq // (2 * t)
    halves = (slice(0, t), slice(t, 2 * t))

    def k_tile(i):
        return k_ref[0, 0, pl.ds(pl.multiple_of(i * 2 * t, 2 * t), 2 * t), :]

    def q_pair(j):
        return qt_ref[0, 0, 2 * j], qt_ref[0, 0, 2 * j + 1]

    def pair_body(j, carry):
        qa0 = pl.multiple_of(2 * j * t, 2 * t)
        qb0 = pl.multiple_of(qa0 + t, t)
        qs = q_pair(j)
        acc_ref[...] = jnp.zeros_like(acc_ref)
        init = jnp.full((1, t), NEG_BIG, F32)

        def full_step(i, ms):
            ms = list(ms)
            vt = vt_ref[0, 0, i]
            k_next = k_tile(i + 1)
            for h, rows in enumerate(halves):
                for c in range(2):
                    ms[c] = update(c, s_ref[h, c], vt[:, rows], ms[c], None)
                    s_ref[h, c] = scores(qs[c], k_next[rows])
            return tuple(ms)

        ms = lax.fori_loop(0, j // 2, lambda i, ms: full_step(2 * i + 1, full_step(2 * i, ms)), (init, init))
        ma, mb = lax.fori_loop(0, j % 2, lambda _, ms: full_step(j - 1, ms), ms)
        vt = vt_ref[0, 0, j]
        q_next = q_pair(jnp.minimum(j + 1, n_pairs - 1))
        k0 = k_tile(0)
        update(0, s_ref[0, 0], vt[:, halves[0]], ma, 0)
        s_ref[0, 0] = scores(q_next[0], k0[halves[0]])
        mb = update(1, s_ref[0, 1], vt[:, halves[0]], mb, None)
        s_ref[0, 1] = scores(q_next[1], k0[halves[0]])
        update(1, s_ref[1, 1], vt[:, halves[1]], mb, 0)
        s_ref[1, 1] = scores(q_next[1], k0[halves[1]])
        s_ref[1, 0] = scores(q_next[0], k0[halves[1]])
        finish(0, qa0)
        finish(1, qb0)
        return carry

    q_first, k_first = q_pair(0), k_tile(0)
    for h, rows in enumerate(halves):
        for c in range(2):
            s_ref[h, c] = scores(q_first[c], k_first[rows])
    lax.fori_loop(0, n_pairs, pair_body, 0)


def _flash(qt, k, vt):
    batch, heads, seq, _ = k.shape
    nk, v_ext, t2 = vt.shape[2:]
    nq, _, t = qt.shape[2:]
    assert nk * t2 == seq and nq * t == seq and t2 == 2 * t
    return pl.pallas_call(
        functools.partial(_flash_kernel, t=t),
        grid=(batch, heads),
        in_specs=[
            pl.BlockSpec((1, 1, nq, HEAD_QK, t), lambda b, h: (b, h, 0, 0, 0)),
            pl.BlockSpec((1, 1, seq, HEAD_QK), lambda b, h: (b, h, 0, 0)),
            pl.BlockSpec((1, 1, nk, v_ext, t2), lambda b, h: (b, h, 0, 0, 0)),
        ],
        out_specs=pl.BlockSpec((seq, V_DIM), lambda b, h: (b, h)),
        out_shape=jax.ShapeDtypeStruct((batch * seq, heads * V_DIM), BF16),
        scratch_shapes=[pltpu.VMEM((2, v_ext, t), F32), pltpu.VMEM((2, 2, t, t), F32)],
        compiler_params=_params("parallel", "parallel"),
        name="mla_attn",
    )(qt, k, vt)


def _merge_kernel(val_ref, gate_ref, oa_ref, oc_ref, g0_ref, g1_ref, g2_ref, x_ref,
                  cw_ref, cb_ref, clng_ref, clnb_ref, wa_ref, wb_ref, wc_ref, wo_ref, gpost_ref,
                  o_ref, ob_ref, ext_ref, sh_ref, y_ref, *, tiles_per_seq):
    s = pl.program_id(0)

    @pl.when(s == 0)
    def _():
        ob_ref[...] = jnp.zeros_like(ob_ref)
        ext_ref[...] = jnp.zeros_like(ext_ref)

    merged = _sigmoid(g0_ref[...].astype(F32)) * _dot(oa_ref[...], wa_ref[...])
    merged += _sigmoid(g1_ref[...].astype(F32)) * _dot(ob_ref[(s + 1) % 2], wb_ref[...])
    merged += _sigmoid(g2_ref[...].astype(F32)) * _dot(oc_ref[...], wc_ref[...])
    m = _dot(merged.astype(BF16), wo_ref[...])
    o_ref[...] = x_ref[...] + _rms(m, gpost_ref[...])

    ob_ref[s % 2] = _conv_tile(val_ref, gate_ref, cw_ref, cb_ref, clng_ref, clnb_ref,
                               ext_ref, sh_ref, y_ref, s % tiles_per_seq == 0)


def _merge(o_a, o_c, p, x, conv_w, conv_b, conv_lng, conv_lnb, w_branch, w_out, gpost, layer, batch):
    t, d = x.shape
    wa, wb, wc = o_a.shape[1], conv_w.shape[2], o_c.shape[1]
    tm = min(MERGE_TM, t // batch)
    n = t // tm
    vblk = 2 * wa // wb
    gblk = (2 * wa + 2 * wb) // d
    once = pl.Buffered(1)
    ahead = lambda s: jnp.minimum(s, n - 1)
    cur = lambda s: jnp.maximum(s - 1, 0)
    vec = lambda w: pl.BlockSpec((None, 1, w), lambda s: (layer, 0, 0))
    return pl.pallas_call(
        functools.partial(_merge_kernel, tiles_per_seq=t // batch // tm),
        grid=(n + 1,),
        in_specs=[
            pl.BlockSpec((tm, wb), lambda s: (ahead(s), vblk)),
            pl.BlockSpec((tm, wb), lambda s: (ahead(s), vblk + 1)),
            pl.BlockSpec((tm, wa), lambda s: (cur(s), 0)),
            pl.BlockSpec((tm, wc), lambda s: (cur(s), 0)),
            pl.BlockSpec((tm, d), lambda s: (cur(s), gblk)),
            pl.BlockSpec((tm, d), lambda s: (cur(s), gblk + 1)),
            pl.BlockSpec((tm, d), lambda s: (cur(s), gblk + 2)),
            pl.BlockSpec((tm, d), lambda s: (cur(s), 0)),
            pl.BlockSpec((None, CONV_K, wb), lambda s: (layer, 0, 0)),
            vec(wb), vec(wb), vec(wb),
            pl.BlockSpec((None, wa, d), lambda s: (layer, 0, 0), pipeline_mode=once),
            pl.BlockSpec((None, wb, d), lambda s: (layer, wa // wb, 0), pipeline_mode=once),
            pl.BlockSpec((None, wc, d), lambda s: (layer, (wa + wb) // wc, 0), pipeline_mode=once),
            pl.BlockSpec((None, d, d), lambda s: (layer, 0, 0), pipeline_mode=once),
            vec(d),
        ],
        out_specs=pl.BlockSpec((tm, d), lambda s: (cur(s), 0)),
        out_shape=jax.ShapeDtypeStruct((t, d), F32),
        scratch_shapes=[
            pltpu.VMEM((2, tm, wb), BF16),
            pltpu.VMEM((tm + CONV_HALO, wb), F32),
            pltpu.VMEM((SUBLANES - 1, tm + CONV_HALO - SUBLANES, LANES), F32),
            pltpu.VMEM((tm, wb), F32),
        ],
        compiler_params=_params("arbitrary"),
        name="merge",
    )(p, p, o_a, o_c, p, p, p, x, conv_w, conv_b, conv_lng, conv_lnb,
      w_branch, w_branch, w_branch, w_out, gpost)


def _rotate_half_columns(w):
    half = w.shape[-1] // 2
    return jnp.concatenate([-w[..., half:], w[..., :half]], axis=-1)


def _mix_in_layout(w_in):
    o = np.cumsum([0, GM_WIDTH, GM_WIDTH, CONV_WIDTH, CONV_WIDTH, Q_RANK, KV_RANK, ROPE_DIM])
    w_in = w_in.astype(BF16)
    k_rope = w_in[..., o[6]:o[7]]
    body = [w_in[..., :o[4]], w_in[..., o[7]:], w_in[..., o[4]:o[6]], k_rope, _rotate_half_columns(k_rope)]
    n = sum(a.shape[-1] for a in body)
    tn = min(PROJ_TN, n)
    pad = (-n) % tn
    body.append(jnp.zeros(w_in.shape[:-1] + (pad,), w_in.dtype))
    return jnp.concatenate(body, axis=-1).astype(BF16)


def _q_layout(w_uq):
    lead = w_uq.shape[:-1]
    w = w_uq.reshape(lead + (N_HEADS, NOPE_DIM + ROPE_DIM))
    rope = w[..., NOPE_DIM:]
    w = jnp.concatenate([w[..., :NOPE_DIM], rope, _rotate_half_columns(rope)], axis=-1)
    w = w.reshape(lead + (N_HEADS * HEAD_QK,))
    return jnp.swapaxes(w, -1, -2).astype(BF16)


def _kv_layout(w_ukv):
    lead = w_ukv.shape[:-1]
    w = w_ukv.reshape(lead + (N_HEADS, NOPE_DIM + V_DIM))
    w_k = w[..., :NOPE_DIM].reshape(lead + (N_HEADS * NOPE_DIM,))
    w_v = w[..., NOPE_DIM:].reshape(lead + (N_HEADS * V_DIM,))
    return w_k.astype(BF16), jnp.swapaxes(w_v, -1, -2).astype(BF16)


def kernel(x, positions, ffn1_norm_pre, ffn1_norm_post, ffn1_w_in, ffn1_w_out, mix_norm_pre, mix_norm_post, mix_w_in, gm_ln_g, gm_ln_b, gm_w_s, gm_b_s, conv_w, conv_b, conv_ln_g, conv_ln_b, mla_q_norm, mla_w_uq, mla_kv_norm, mla_w_ukv, mix_w_branch, mix_w_out, ffn2_norm_pre, ffn2_norm_post, ffn2_w_in, ffn2_w_out):
    batch, seq, d = x.shape
    depth = ffn1_w_in.shape[0]
    t = batch * seq
    row = lambda a: a[:, None, :]

    inv_freq = ROPE_THETA ** (-jnp.arange(0, ROPE_DIM, 2, dtype=F32) / ROPE_DIM)
    freq4 = jnp.tile(inv_freq, 4)[None, :]
    cs, cst = _rope_table(positions.astype(F32).reshape(t), freq4)

    w_mix = _mix_in_layout(mix_w_in)
    w_qt = _q_layout(mla_w_uq)
    w_k, w_vt = _kv_layout(mla_w_ukv)
    w_branch = mix_w_branch.astype(BF16)
    w_out = mix_w_out.astype(BF16)
    f1_in, f1_out = ffn1_w_in.astype(BF16), ffn1_w_out.astype(BF16)
    f2_in, f2_out = ffn2_w_in.astype(BF16), ffn2_w_out.astype(BF16)
    gm_bias = jnp.repeat(jnp.swapaxes(gm_b_s, 1, 2), GM_WIDTH // GM_GROUPS, axis=2)

    gates_end = 2 * GM_WIDTH + 2 * CONV_WIDTH + 3 * d
    cq_blk = gates_end // Q_RANK
    ckv_blk = (gates_end + Q_RANK) // KV_RANK
    kr_blk = (gates_end + Q_RANK + KV_RANK) // (2 * ROPE_DIM)

    xf = x.reshape(t, d)
    for l in range(depth):
        xf = _ffn(xf, row(ffn1_norm_pre), f1_in, f1_out, row(ffn1_norm_post), l)
        p = _proj(xf, row(mix_norm_pre), w_mix, l)
        o_a = _gmlp(p, row(gm_ln_g), row(gm_ln_b), gm_w_s, gm_bias, l)
        qt = _q_proj(p, cst, row(mla_q_norm), w_qt, l, batch, cq_blk)
        k, vt = _kv_proj(p, cs, row(mla_kv_norm), w_k, w_vt, l, batch, ckv_blk, kr_blk)
        o_c = _flash(qt, k, vt)
        xf = _merge(o_a, o_c, p, xf, conv_w, row(conv_b), row(conv_ln_g), row(conv_ln_b),
                    w_branch, w_out, row(mix_norm_post), l, batch)
        xf = _ffn(xf, row(ffn2_norm_pre), f2_in, f2_out, row(ffn2_norm_post), l)
    return xf.reshape(batch, seq, d)
```

```python
import functools
import math

import jax
import jax.numpy as jnp
import numpy as np
from jax import lax
from jax.experimental import pallas as pl
from jax.experimental.pallas import tpu as pltpu

D_MODEL = 2048
D_FF = 4096
GM_WIDTH = 1024
GM_GROUPS = 4
CHUNK = 128
CONV_WIDTH = 1024
CONV_K = 31
N_HEADS = 16
Q_RANK = 512
KV_RANK = 512
NOPE_DIM = 128
ROPE_DIM = 64
V_DIM = 128
ROPE_THETA = 10000.0
EPS = 1e-6

LANES = 128
SUBLANES = 8
CONV_RB = 64
NORM_ROWS = 16
HEAD_QK = NOPE_DIM + 2 * ROPE_DIM
V_EXT = V_DIM + 16
CONV_HALO = 32

FFN_TM = 512
FFN_TF = 1024
PROJ_TM = 1024
PROJ_TN = 2304
GM_TM = 512
ATT_T = 512
MERGE_TM = 256
ROPE_TM = 1024

VMEM_LIMIT = 56 * 1024 * 1024

F32 = jnp.float32
BF16 = jnp.bfloat16
NEG_BIG = -1e30


def _params(*sem):
    return pltpu.CompilerParams(dimension_semantics=sem, vmem_limit_bytes=VMEM_LIMIT)


def _rms(x, g):
    return x * lax.rsqrt(jnp.mean(x * x, axis=-1, keepdims=True) + EPS) * g


def _for_row_chunks(n_rows, fn):
    for r0 in range(0, n_rows, NORM_ROWS):
        fn(slice(r0, r0 + NORM_ROWS))


def _layer_norm(x, g, b):
    mu = jnp.mean(x, axis=-1, keepdims=True)
    xc = x - mu
    return xc * lax.rsqrt(jnp.mean(xc * xc, axis=-1, keepdims=True) + EPS) * g + b


def _sigmoid(x):
    return 1.0 / (1.0 + jnp.exp(-x))


def _gelu_tanh(x):
    c = math.sqrt(2.0 / math.pi)
    return x * (0.5 * (1.0 + jnp.tanh(c * (x + 0.044715 * (x * x * x)))))


def _dot(a, b):
    return jnp.dot(a, b, preferred_element_type=F32)


def _rope_table_kernel(pos_col_ref, pos_row_ref, freq_row_ref, freq_col_ref, o_ref, ot_ref):
    def table(ang, axis):
        idx = lax.broadcasted_iota(jnp.int32, ang.shape, axis)
        return jnp.where(idx < ROPE_DIM, jnp.cos(ang), jnp.sin(ang))

    o_ref[...] = table(pos_col_ref[...] * freq_row_ref[...], 1)
    ot_ref[...] = table(freq_col_ref[...] * pos_row_ref[...], 0)


def _rope_table(pos_f, freq4):
    t = pos_f.shape[0]
    tm = min(ROPE_TM, t)
    return pl.pallas_call(
        _rope_table_kernel,
        grid=(t // tm,),
        in_specs=[pl.BlockSpec((tm, 1), lambda i: (i, 0)),
                  pl.BlockSpec((1, tm), lambda i: (0, i)),
                  pl.BlockSpec((1, 2 * ROPE_DIM), lambda i: (0, 0)),
                  pl.BlockSpec((2 * ROPE_DIM, 1), lambda i: (0, 0))],
        out_specs=[pl.BlockSpec((tm, 2 * ROPE_DIM), lambda i: (i, 0)),
                   pl.BlockSpec((2 * ROPE_DIM, tm), lambda i: (0, i))],
        out_shape=[jax.ShapeDtypeStruct((t, 2 * ROPE_DIM), F32),
                   jax.ShapeDtypeStruct((2 * ROPE_DIM, t), F32)],
        compiler_params=_params("parallel"),
        name="rope_table",
    )(pos_f.reshape(t, 1), pos_f.reshape(1, t), freq4.reshape(1, -1), freq4.reshape(-1, 1))


def _ffn_kernel(x_ref, gpre_ref, wg_ref, wu_ref, wo_ref, gpost_ref, o_ref, h_ref, acc_ref):
    f = pl.program_id(1)

    @pl.when(f == 0)
    def _():
        def norm_rows(rows):
            h_ref[rows, :] = _rms(x_ref[rows, :], gpre_ref[...]).astype(BF16)

        _for_row_chunks(x_ref.shape[0], norm_rows)
        acc_ref[...] = jnp.zeros_like(acc_ref)

    h = h_ref[...]
    gate = _dot(h, wg_ref[...])
    up = _dot(h, wu_ref[...])
    act = (gate * _sigmoid(gate) * up).astype(BF16)
    acc_ref[...] += _dot(act, wo_ref[...])

    @pl.when(f == pl.num_programs(1) - 1)
    def _():
        def finish_rows(rows):
            o_ref[rows, :] = x_ref[rows, :] + 0.5 * _rms(acc_ref[rows, :], gpost_ref[...])

        _for_row_chunks(x_ref.shape[0], finish_rows)


def _ffn(x, gpre, w_in, w_out, gpost, layer):
    t, d = x.shape
    d_ff = w_out.shape[1]
    tm, tf = min(FFN_TM, t), min(FFN_TF, d_ff)
    nf = d_ff // tf
    return pl.pallas_call(
        _ffn_kernel,
        grid=(t // tm, nf),
        in_specs=[
            pl.BlockSpec((tm, d), lambda i, f: (i, 0)),
            pl.BlockSpec((None, 1, d), lambda i, f: (layer, 0, 0)),
            pl.BlockSpec((None, d, tf), lambda i, f: (layer, 0, f)),
            pl.BlockSpec((None, d, tf), lambda i, f: (layer, 0, f + nf)),
            pl.BlockSpec((None, tf, d), lambda i, f: (layer, f, 0)),
            pl.BlockSpec((None, 1, d), lambda i, f: (layer, 0, 0)),
        ],
        out_specs=pl.BlockSpec((tm, d), lambda i, f: (i, 0)),
        out_shape=jax.ShapeDtypeStruct((t, d), F32),
        scratch_shapes=[pltpu.VMEM((tm, d), BF16), pltpu.VMEM((tm, d), F32)],
        compiler_params=_params("parallel", "arbitrary"),
        name="ffn",
    )(x, gpre, w_in, w_in, w_out, gpost)


def _proj_kernel(x_ref, g_ref, w_ref, o_ref, h_ref):
    @pl.when(pl.program_id(1) == 0)
    def _():
        def norm_rows(rows):
            h_ref[rows, :] = _rms(x_ref[rows, :], g_ref[...]).astype(BF16)

        _for_row_chunks(x_ref.shape[0], norm_rows)

    o_ref[...] = _dot(h_ref[...], w_ref[...]).astype(BF16)


def _proj(x, g, w_ext, layer):
    t, d = x.shape
    n = w_ext.shape[2]
    tm, tn = min(PROJ_TM, t), min(PROJ_TN, n)
    return pl.pallas_call(
        _proj_kernel,
        grid=(t // tm, n // tn),
        in_specs=[
            pl.BlockSpec((tm, d), lambda i, j: (i, 0)),
            pl.BlockSpec((None, 1, d), lambda i, j: (layer, 0, 0)),
            pl.BlockSpec((None, d, tn), lambda i, j: (layer, 0, j)),
        ],
        out_specs=pl.BlockSpec((tm, tn), lambda i, j: (i, j)),
        out_shape=jax.ShapeDtypeStruct((t, n), BF16),
        scratch_shapes=[pltpu.VMEM((tm, d), BF16)],
        compiler_params=_params("parallel", "arbitrary"),
        name="mix_proj",
    )(x, g, w_ext)


def _gmlp_kernel(u_ref, v_ref, lng_ref, lnb_ref, ws_ref, bias_ref, o_ref):
    tm, width = u_ref.shape
    gw = width // GM_GROUPS
    vn = _layer_norm(_gelu_tanh(v_ref[...].astype(F32)), lng_ref[...], lnb_ref[...]).astype(BF16)
    row = lax.broadcasted_iota(jnp.int32, (CHUNK, CHUNK), 0)
    col = lax.broadcasted_iota(jnp.int32, (CHUNK, CHUNK), 1)
    for g in range(GM_GROUPS):
        w = jnp.where(col <= row, ws_ref[g], 0.0).astype(BF16)
        cols = slice(g * gw, (g + 1) * gw)
        for c in range(tm // CHUNK):
            rows = slice(c * CHUNK, (c + 1) * CHUNK)
            s = _dot(w, vn[rows, cols]) + bias_ref[:, cols]
            o_ref[rows, cols] = (_gelu_tanh(u_ref[rows, cols].astype(F32)) * s).astype(BF16)


def _gmlp(p, lng, lnb, w_s, bias_full, layer):
    t = p.shape[0]
    width = lng.shape[2]
    tm = min(GM_TM, t)
    return pl.pallas_call(
        _gmlp_kernel,
        grid=(t // tm,),
        in_specs=[
            pl.BlockSpec((tm, width), lambda i: (i, 0)),
            pl.BlockSpec((tm, width), lambda i: (i, 1)),
            pl.BlockSpec((None, 1, width), lambda i: (layer, 0, 0)),
            pl.BlockSpec((None, 1, width), lambda i: (layer, 0, 0)),
            pl.BlockSpec((None, GM_GROUPS, CHUNK, CHUNK), lambda i: (layer, 0, 0, 0)),
            pl.BlockSpec((None, CHUNK, width), lambda i: (layer, 0, 0)),
        ],
        out_specs=pl.BlockSpec((tm, width), lambda i: (i, 0)),
        out_shape=jax.ShapeDtypeStruct((t, width), BF16),
        compiler_params=_params("parallel"),
        name="gmlp",
    )(p, p, lng, lnb, w_s, bias_full)


def _conv_tile(val_ref, gate_ref, w_ref, b_ref, lng_ref, lnb_ref, ext_ref, sh_ref, y_ref, first_of_seq):
    tm, width = val_ref.shape
    ext_ref[0:CONV_HALO, :] = jnp.where(first_of_seq, 0.0, ext_ref[tm:tm + CONV_HALO, :])
    ext_ref[CONV_HALO:CONV_HALO + tm, :] = val_ref[...].astype(F32) * _sigmoid(gate_ref[...].astype(F32))
    first = CONV_HALO - (CONV_K - 1)
    span = sh_ref.shape[1]
    rows = min(CONV_RB, tm)
    for c in range(width // LANES):
        lanes = slice(c * LANES, (c + 1) * LANES)
        for r in range(1, SUBLANES):
            sh_ref[r - 1] = ext_ref[r:r + span, lanes]
        for rb in range(tm // rows):
            acc = jnp.broadcast_to(b_ref[:, lanes], (rows, LANES))
            for k in range(CONV_K):
                r = (first + k) % SUBLANES
                row0 = first + k - r + rb * rows
                src = ext_ref[row0:row0 + rows, lanes] if r == 0 else sh_ref[r - 1, row0:row0 + rows, :]
                acc = acc + w_ref[k:k + 1, lanes] * src
            y_ref[rb * rows:(rb + 1) * rows, lanes] = acc
    z = _layer_norm(y_ref[...], lng_ref[...], lnb_ref[...])
    return (z * _sigmoid(z)).astype(BF16)


def _rope_pair(pair, cs):
    z = pair * cs
    return z + pltpu.roll(z, ROPE_DIM, axis=1)


def _q_kernel(c_ref, cst_ref, g_ref, wt_ref, o_ref, *, q_scale):
    cn = _rms(c_ref[...].astype(F32), g_ref[...]).astype(BF16)
    qt = lax.dot_general(wt_ref[...], cn, (((1,), (1,)), ((), ())), preferred_element_type=F32) * q_scale
    cst = cst_ref[...]
    for h in range(o_ref.shape[1]):
        base = h * HEAD_QK
        o_ref[0, h, 0, 0:NOPE_DIM, :] = qt[base:base + NOPE_DIM].astype(BF16)
        z = qt[base + NOPE_DIM:base + HEAD_QK] * cst
        rot = (z[0:ROPE_DIM] + z[ROPE_DIM:2 * ROPE_DIM]).astype(BF16)
        o_ref[0, h, 0, NOPE_DIM:NOPE_DIM + ROPE_DIM, :] = rot
        o_ref[0, h, 0, NOPE_DIM + ROPE_DIM:HEAD_QK, :] = jnp.zeros_like(rot)


def _kv_kernel(c_ref, kr_ref, cs_ref, g_ref, wk_ref, wvt_ref, k_ref, vt_ref):
    cn = _rms(c_ref[...].astype(F32), g_ref[...]).astype(BF16)
    kn = _dot(cn, wk_ref[...])
    vt = lax.dot_general(wvt_ref[...], cn, (((1,), (1,)), ((), ())), preferred_element_type=F32)
    rot = _rope_pair(kr_ref[...].astype(F32), cs_ref[...])
    lane = lax.broadcasted_iota(jnp.int32, rot.shape, 1)
    rot = jnp.where(lane < ROPE_DIM, rot, 0.0).astype(BF16)
    for h in range(k_ref.shape[1]):
        k_ref[0, h, :, 0:NOPE_DIM] = kn[:, h * NOPE_DIM:(h + 1) * NOPE_DIM].astype(BF16)
        k_ref[0, h, :, NOPE_DIM:HEAD_QK] = rot
        vt_ref[0, h, 0, 0:V_DIM, :] = vt[h * V_DIM:(h + 1) * V_DIM, :].astype(BF16)
        vt_ref[0, h, 0, V_DIM:V_EXT, :] = jnp.ones((V_EXT - V_DIM, vt.shape[1]), BF16)


def _q_proj(p, cst, g, w_qt, layer, batch, col_blk):
    t = p.shape[0]
    rank = w_qt.shape[2]
    heads = w_qt.shape[1] // HEAD_QK
    seq = t // batch
    tm = min(ATT_T, seq)
    ns = seq // tm
    q_scale = float((NOPE_DIM + ROPE_DIM) ** -0.5 * math.log2(math.e))
    return pl.pallas_call(
        functools.partial(_q_kernel, q_scale=q_scale),
        grid=(batch, ns),
        in_specs=[
            pl.BlockSpec((tm, rank), lambda bi, i: (bi * ns + i, col_blk)),
            pl.BlockSpec((2 * ROPE_DIM, tm), lambda bi, i: (0, bi * ns + i)),
            pl.BlockSpec((None, 1, rank), lambda bi, i: (layer, 0, 0)),
            pl.BlockSpec((None, heads * HEAD_QK, rank), lambda bi, i: (layer, 0, 0)),
        ],
        out_specs=pl.BlockSpec((1, heads, 1, HEAD_QK, tm), lambda bi, i: (bi, 0, i, 0, 0)),
        out_shape=jax.ShapeDtypeStruct((batch, heads, ns, HEAD_QK, tm), BF16),
        compiler_params=_params("parallel", "parallel"),
        name="mla_q",
    )(p, cst, g, w_qt)


def _kv_proj(p, cs, g, w_k, w_vt, layer, batch, col_blk, kr_blk):
    t = p.shape[0]
    rank = w_k.shape[1]
    heads = w_k.shape[2] // NOPE_DIM
    seq = t // batch
    tm = min(ATT_T, seq)
    ns = seq // tm
    assert ns % 2 == 0
    return pl.pallas_call(
        _kv_kernel,
        grid=(batch, ns),
        in_specs=[
            pl.BlockSpec((tm, rank), lambda bi, i: (bi * ns + i, col_blk)),
            pl.BlockSpec((tm, 2 * ROPE_DIM), lambda bi, i: (bi * ns + i, kr_blk)),
            pl.BlockSpec((tm, 2 * ROPE_DIM), lambda bi, i: (bi * ns + i, 0)),
            pl.BlockSpec((None, 1, rank), lambda bi, i: (layer, 0, 0)),
            pl.BlockSpec((None, rank, heads * NOPE_DIM), lambda bi, i: (layer, 0, 0)),
            pl.BlockSpec((None, heads * V_DIM, rank), lambda bi, i: (layer, 0, 0)),
        ],
        out_specs=[
            pl.BlockSpec((1, heads, tm, HEAD_QK), lambda bi, i: (bi, 0, i, 0)),
            pl.BlockSpec((1, heads, 1, V_EXT, tm), lambda bi, i: (bi, 0, i // 2, 0, i % 2)),
        ],
        out_shape=[
            jax.ShapeDtypeStruct((batch, heads, seq, HEAD_QK), BF16),
            jax.ShapeDtypeStruct((batch, heads, ns // 2, V_EXT, 2 * tm), BF16),
        ],
        compiler_params=_params("parallel", "parallel"),
        name="mla_kv",
    )(p, p, cs, g, w_k, w_vt)


def _flash_kernel(qt_ref, k_ref, vt_ref, o_ref, acc_ref, s_ref, *, t):
    seq = k_ref.shape[2]

    def scores(qt, k):
        return _dot(k, qt)

    def update(chain, s, vt, m, diag_offset):
        if diag_offset is not None:
            key = lax.broadcasted_iota(jnp.int32, s.shape, 0)
            qry = lax.broadcasted_iota(jnp.int32, s.shape, 1)
            s = jnp.where(key <= qry + diag_offset, s, NEG_BIG)
        m_new = jnp.maximum(m, jnp.max(s, axis=0, keepdims=True))
        pr = jnp.exp2(s - m_new).astype(BF16)
        acc_ref[chain] = jnp.exp2(m - m_new) * acc_ref[chain] + _dot(vt, pr)
        return m_new

    def finish(chain, q0):
        o = acc_ref[chain, 0:V_DIM, :] / acc_ref[chain, V_DIM:V_DIM + 1, :]
        o_ref[pl.ds(q0, t), :] = o.T.astype(BF16)

    n_pairs = seq // (2 * t)
    halves = (slice(0, t), slice(t, 2 * t))

    def k_tile(i):
        return k_ref[0, 0, pl.ds(pl.multiple_of(i * 2 * t, 2 * t), 2 * t), :]

    def q_pair(j):
        return qt_ref[0, 0, 2 * j], qt_ref[0, 0, 2 * j + 1]

    def pair_body(j, carry):
        qa0 = pl.multiple_of(2 * j * t, 2 * t)
        qb0 = pl.multiple_of(qa0 + t, t)
        qs = q_pair(j)
        acc_ref[...] = jnp.zeros_like(acc_ref)
        init = jnp.full((1, t), NEG_BIG, F32)

        def full_step(i, ms):
            ms = list(ms)
            vt = vt_ref[0, 0, i]
            k_next = k_tile(i + 1)
            for h, rows in enumerate(halves):
                for c in range(2):
                    ms[c] = update(c, s_ref[h, c], vt[:, rows], ms[c], None)
                    s_ref[h, c] = scores(qs[c], k_next[rows])
            return tuple(ms)

        ms = lax.fori_loop(0, j // 2, lambda i, ms: full_step(2 * i + 1, full_step(2 * i, ms)), (init, init))
        ma, mb = lax.fori_loop(0, j % 2, lambda _, ms: full_step(j - 1, ms), ms)
        vt = vt_ref[0, 0, j]
        q_next = q_pair(jnp.minimum(j + 1, n_pairs - 1))
        k0 = k_tile(0)
        update(0, s_ref[0, 0], vt[:, halves[0]], ma, 0)
        s_ref[0, 0] = scores(q_next[0], k0[halves[0]])
        mb = update(1, s_ref[0, 1], vt[:, halves[0]], mb, None)
        s_ref[0, 1] = scores(q_next[1], k0[halves[0]])
        update(1, s_ref[1, 1], vt[:, halves[1]], mb, 0)
        s_ref[1, 1] = scores(q_next[1], k0[halves[1]])
        s_ref[1, 0] = scores(q_next[0], k0[halves[1]])
        finish(0, qa0)
        finish(1, qb0)
        return carry

    q_first, k_first = q_pair(0), k_tile(0)
    for h, rows in enumerate(halves):
        for c in range(2):
            s_ref[h, c] = scores(q_first[c], k_first[rows])
    lax.fori_loop(0, n_pairs, pair_body, 0)


def _flash(qt, k, vt):
    batch, heads, seq, _ = k.shape
    nk, v_ext, t2 = vt.shape[2:]
    nq, _, t = qt.shape[2:]
    assert nk * t2 == seq and nq * t == seq and t2 == 2 * t
    return pl.pallas_call(
        functools.partial(_flash_kernel, t=t),
        grid=(batch, heads),
        in_specs=[
            pl.BlockSpec((1, 1, nq, HEAD_QK, t), lambda b, h: (b, h, 0, 0, 0)),
            pl.BlockSpec((1, 1, seq, HEAD_QK), lambda b, h: (b, h, 0, 0)),
            pl.BlockSpec((1, 1, nk, v_ext, t2), lambda b, h: (b, h, 0, 0, 0)),
        ],
        out_specs=pl.BlockSpec((seq, V_DIM), lambda b, h: (b, h)),
        out_shape=jax.ShapeDtypeStruct((batch * seq, heads * V_DIM), BF16),
        scratch_shapes=[pltpu.VMEM((2, v_ext, t), F32), pltpu.VMEM((2, 2, t, t), F32)],
        compiler_params=_params("parallel", "parallel"),
        name="mla_attn",
    )(qt, k, vt)


def _merge_kernel(val_ref, gate_ref, oa_ref, oc_ref, g0_ref, g1_ref, g2_ref, x_ref,
                  cw_ref, cb_ref, clng_ref, clnb_ref, wa_ref, wb_ref, wc_ref, wo_ref, gpost_ref,
                  o_ref, ob_ref, ext_ref, sh_ref, y_ref, *, tiles_per_seq):
    s = pl.program_id(0)

    @pl.when(s == 0)
    def _():
        ob_ref[...] = jnp.zeros_like(ob_ref)
        ext_ref[...] = jnp.zeros_like(ext_ref)

    merged = _sigmoid(g0_ref[...].astype(F32)) * _dot(oa_ref[...], wa_ref[...])
    merged += _sigmoid(g1_ref[...].astype(F32)) * _dot(ob_ref[(s + 1) % 2], wb_ref[...])
    merged += _sigmoid(g2_ref[...].astype(F32)) * _dot(oc_ref[...], wc_ref[...])
    m = _dot(merged.astype(BF16), wo_ref[...])
    o_ref[...] = x_ref[...] + _rms(m, gpost_ref[...])

    ob_ref[s % 2] = _conv_tile(val_ref, gate_ref, cw_ref, cb_ref, clng_ref, clnb_ref,
                               ext_ref, sh_ref, y_ref, s % tiles_per_seq == 0)


def _merge(o_a, o_c, p, x, conv_w, conv_b, conv_lng, conv_lnb, w_branch, w_out, gpost, layer, batch):
    t, d = x.shape
    wa, wb, wc = o_a.shape[1], conv_w.shape[2], o_c.shape[1]
    tm = min(MERGE_TM, t // batch)
    n = t // tm
    vblk = 2 * wa // wb
    gblk = (2 * wa + 2 * wb) // d
    once = pl.Buffered(1)
    ahead = lambda s: jnp.minimum(s, n - 1)
    cur = lambda s: jnp.maximum(s - 1, 0)
    vec = lambda w: pl.BlockSpec((None, 1, w), lambda s: (layer, 0, 0))
    return pl.pallas_call(
        functools.partial(_merge_kernel, tiles_per_seq=t // batch // tm),
        grid=(n + 1,),
        in_specs=[
            pl.BlockSpec((tm, wb), lambda s: (ahead(s), vblk)),
            pl.BlockSpec((tm, wb), lambda s: (ahead(s), vblk + 1)),
            pl.BlockSpec((tm, wa), lambda s: (cur(s), 0)),
            pl.BlockSpec((tm, wc), lambda s: (cur(s), 0)),
            pl.BlockSpec((tm, d), lambda s: (cur(s), gblk)),
            pl.BlockSpec((tm, d), lambda s: (cur(s), gblk + 1)),
            pl.BlockSpec((tm, d), lambda s: (cur(s), gblk + 2)),
            pl.BlockSpec((tm, d), lambda s: (cur(s), 0)),
            pl.BlockSpec((None, CONV_K, wb), lambda s: (layer, 0, 0)),
            vec(wb), vec(wb), vec(wb),
            pl.BlockSpec((None, wa, d), lambda s: (layer, 0, 0), pipeline_mode=once),
            pl.BlockSpec((None, wb, d), lambda s: (layer, wa // wb, 0), pipeline_mode=once),
            pl.BlockSpec((None, wc, d), lambda s: (layer, (wa + wb) // wc, 0), pipeline_mode=once),
            pl.BlockSpec((None, d, d), lambda s: (layer, 0, 0), pipeline_mode=once),
            vec(d),
        ],
        out_specs=pl.BlockSpec((tm, d), lambda s: (cur(s), 0)),
        out_shape=jax.ShapeDtypeStruct((t, d), F32),
        scratch_shapes=[
            pltpu.VMEM((2, tm, wb), BF16),
            pltpu.VMEM((tm + CONV_HALO, wb), F32),
            pltpu.VMEM((SUBLANES - 1, tm + CONV_HALO - SUBLANES, LANES), F32),
            pltpu.VMEM((tm, wb), F32),
        ],
        compiler_params=_params("arbitrary"),
        name="merge",
    )(p, p, o_a, o_c, p, p, p, x, conv_w, conv_b, conv_lng, conv_lnb,
      w_branch, w_branch, w_branch, w_out, gpost)


def _rotate_half_columns(w):
    half = w.shape[-1] // 2
    return jnp.concatenate([-w[..., half:], w[..., :half]], axis=-1)


def _mix_in_layout(w_in):
    o = np.cumsum([0, GM_WIDTH, GM_WIDTH, CONV_WIDTH, CONV_WIDTH, Q_RANK, KV_RANK, ROPE_DIM])
    w_in = w_in.astype(BF16)
    k_rope = w_in[..., o[6]:o[7]]
    body = [w_in[..., :o[4]], w_in[..., o[7]:], w_in[..., o[4]:o[6]], k_rope, _rotate_half_columns(k_rope)]
    n = sum(a.shape[-1] for a in body)
    tn = min(PROJ_TN, n)
    pad = (-n) % tn
    body.append(jnp.zeros(w_in.shape[:-1] + (pad,), w_in.dtype))
    return jnp.concatenate(body, axis=-1).astype(BF16)


def _q_layout(w_uq):
    lead = w_uq.shape[:-1]
    w = w_uq.reshape(lead + (N_HEADS, NOPE_DIM + ROPE_DIM))
    rope = w[..., NOPE_DIM:]
    w = jnp.concatenate([w[..., :NOPE_DIM], rope, _rotate_half_columns(rope)], axis=-1)
    w = w.reshape(lead + (N_HEADS * HEAD_QK,))
    return jnp.swapaxes(w, -1, -2).astype(BF16)


def _kv_layout(w_ukv):
    lead = w_ukv.shape[:-1]
    w = w_ukv.reshape(lead + (N_HEADS, NOPE_DIM + V_DIM))
    w_k = w[..., :NOPE_DIM].reshape(lead + (N_HEADS * NOPE_DIM,))
    w_v = w[..., NOPE_DIM:].reshape(lead + (N_HEADS * V_DIM,))
    return w_k.astype(BF16), jnp.swapaxes(w_v, -1, -2).astype(BF16)


def kernel(x, positions, ffn1_norm_pre, ffn1_norm_post, ffn1_w_in, ffn1_w_out, mix_norm_pre, mix_norm_post, mix_w_in, gm_ln_g, gm_ln_b, gm_w_s, gm_b_s, conv_w, conv_b, conv_ln_g, conv_ln_b, mla_q_norm, mla_w_uq, mla_kv_norm, mla_w_ukv, mix_w_branch, mix_w_out, ffn2_norm_pre, ffn2_norm_post, ffn2_w_in, ffn2_w_out):
    batch, seq, d = x.shape
    depth = ffn1_w_in.shape[0]
    t = batch * seq
    row = lambda a: a[:, None, :]

    inv_freq = ROPE_THETA ** (-jnp.arange(0, ROPE_DIM, 2, dtype=F32) / ROPE_DIM)
    freq4 = jnp.tile(inv_freq, 4)[None, :]
    cs, cst = _rope_table(positions.astype(F32).reshape(t), freq4)

    w_mix = _mix_in_layout(mix_w_in)
    w_qt = _q_layout(mla_w_uq)
    w_k, w_vt = _kv_layout(mla_w_ukv)
    w_branch = mix_w_branch.astype(BF16)
    w_out = mix_w_out.astype(BF16)
    f1_in, f1_out = ffn1_w_in.astype(BF16), ffn1_w_out.astype(BF16)
    f2_in, f2_out = ffn2_w_in.astype(BF16), ffn2_w_out.astype(BF16)
    gm_bias = jnp.repeat(jnp.swapaxes(gm_b_s, 1, 2), GM_WIDTH // GM_GROUPS, axis=2)

    gates_end = 2 * GM_WIDTH + 2 * CONV_WIDTH + 3 * d
    cq_blk = gates_end // Q_RANK
    ckv_blk = (gates_end + Q_RANK) // KV_RANK
    kr_blk = (gates_end + Q_RANK + KV_RANK) // (2 * ROPE_DIM)

    xf = x.reshape(t, d)
    for l in range(depth):
        xf = _ffn(xf, row(ffn1_norm_pre), f1_in, f1_out, row(ffn1_norm_post), l)
        p = _proj(xf, row(mix_norm_pre), w_mix, l)
        o_a = _gmlp(p, row(gm_ln_g), row(gm_ln_b), gm_w_s, gm_bias, l)
        qt = _q_proj(p, cst, row(mla_q_norm), w_qt, l, batch, cq_blk)
        k, vt = _kv_proj(p, cs, row(mla_kv_norm), w_k, w_vt, l, batch, ckv_blk, kr_blk)
        o_c = _flash(qt, k, vt)
        xf = _merge(o_a, o_c, p, xf, conv_w, row(conv_b), row(conv_ln_g), row(conv_ln_b),
                    w_branch, w_out, row(mix_norm_post), l, batch)
        xf = _ffn(xf, row(ffn2_norm_pre), f2_in, f2_out, row(ffn2_norm_post), l)
    return xf.reshape(batch, seq, d)
```
